```python
import jax, jax.numpy as jnp
from jax import lax
import numpy as np


D_MODEL = 1024
BATCH = 4
SEQ = 8192
DEPTH = 4

MLA_HEADS = 4
QK_NOPE_DIM = 64
QK_ROPE_DIM = 32
V_HEAD_DIM = 64
Q_RANK = D_MODEL // 4
KV_RANK = D_MODEL // 8
ROPE_BASE = 10000.0
Q_BLOCK = 128
SG_GROUPS = 4
SG_WIDTH = D_MODEL // 4
SG_CHUNK = 128
CONV_WIDTH = D_MODEL // 4
CONV_K = 3
POOL_WINDOWS = (2, 4, 8, 16)
POOL_WIDTH = D_MODEL // 4
POOL_GROUP = POOL_WIDTH // 4
N_BRANCH = 4
D_FF = 4 * D_MODEL
EPS = 1e-6
N_IN = Q_RANK + KV_RANK + QK_ROPE_DIM + 2 * SG_WIDTH + 3 * CONV_WIDTH + POOL_WIDTH + N_BRANCH * D_MODEL

kernel_name = 'hybrid_gated_mla_sgmlp_conv_pool_block'


def rmsnorm(x, g):
    xf = x.astype(jnp.float32)
    y = xf * lax.rsqrt(jnp.mean(xf * xf, axis=-1, keepdims=True) + EPS)
    return (y * g.astype(jnp.float32)).astype(x.dtype)


def layernorm(x, g, b):
    xf = x.astype(jnp.float32)
    mu = jnp.mean(xf, axis=-1, keepdims=True)
    xc = xf - mu
    y = xc * lax.rsqrt(jnp.mean(xc * xc, axis=-1, keepdims=True) + EPS)
    return (y * g.astype(jnp.float32) + b.astype(jnp.float32)).astype(x.dtype)


def split_cols(proj):
    sizes = (Q_RANK, KV_RANK, QK_ROPE_DIM, SG_WIDTH, SG_WIDTH, CONV_WIDTH, CONV_WIDTH, CONV_WIDTH,
             POOL_WIDTH, N_BRANCH * D_MODEL)
    offs = []
    acc = 0
    for s in sizes[:-1]:
        acc += s
        offs.append(acc)
    return jnp.split(proj, offs, axis=-1)


def rope(x, cos, sin):
    x1, x2 = jnp.split(x, 2, axis=-1)
    return jnp.concatenate([x1 * cos - x2 * sin, x2 * cos + x1 * sin], axis=-1)


def mla(c_q, c_kv, k_rope, positions, q_norm, w_uq, kv_norm, w_ukv):
    B_, S_, _ = c_q.shape
    q = (rmsnorm(c_q, q_norm) @ w_uq).reshape(B_, S_, MLA_HEADS, QK_NOPE_DIM + QK_ROPE_DIM)
    q_nope, q_rope = q[..., :QK_NOPE_DIM], q[..., QK_NOPE_DIM:]
    kv = (rmsnorm(c_kv, kv_norm) @ w_ukv).reshape(B_, S_, MLA_HEADS, QK_NOPE_DIM + V_HEAD_DIM)
    k_nope, v = kv[..., :QK_NOPE_DIM], kv[..., QK_NOPE_DIM:]
    inv_freq = ROPE_BASE ** (-jnp.arange(0, QK_ROPE_DIM, 2, dtype=jnp.float32) / QK_ROPE_DIM)
    ang = positions.astype(jnp.float32)[..., None] * inv_freq
    cos = jnp.cos(ang).astype(q.dtype)
    sin = jnp.sin(ang).astype(q.dtype)
    q_rope = rope(q_rope, cos[:, :, None, :], sin[:, :, None, :])
    k_rope = rope(k_rope, cos, sin)
    scale = (QK_NOPE_DIM + QK_ROPE_DIM) ** -0.5
    nb = S_ // Q_BLOCK

    def to_blocks(a):
        return jnp.moveaxis(a.reshape((B_, nb, Q_BLOCK) + a.shape[2:]), 1, 0)

    k_idx = jnp.arange(S_)

    def block(args):
        qn, qr, start = args
        s = jnp.einsum('bqhd,bkhd->bhqk', qn, k_nope) + jnp.einsum('bqhr,bkr->bhqk', qr, k_rope)
        s = s.astype(jnp.float32) * scale
        q_idx = start + jnp.arange(Q_BLOCK)
        s = jnp.where(k_idx[None, :] <= q_idx[:, None], s, -jnp.inf)
        p = jax.nn.softmax(s, axis=-1).astype(v.dtype)
        return jnp.einsum('bhqk,bkhd->bqhd', p, v)

    starts = jnp.arange(nb, dtype=jnp.int32) * Q_BLOCK
    o = lax.map(block, (to_blocks(q_nope), to_blocks(q_rope), starts))
    return jnp.moveaxis(o, 0, 1).reshape(B_, S_, MLA_HEADS * V_HEAD_DIM)


def spatial_gating(u, v, ln_g, ln_b, w_s, b_s):
    B_, S_, _ = u.shape
    u = jax.nn.gelu(u)
    v = layernorm(jax.nn.gelu(v), ln_g, ln_b)
    n = S_ // SG_CHUNK
    vc = v.reshape(B_, n, SG_CHUNK, SG_GROUPS, SG_WIDTH // SG_GROUPS)
    mask = jnp.tril(jnp.ones((SG_CHUNK, SG_CHUNK), dtype=bool))
    w = jnp.where(mask, w_s, 0)
    s = jnp.einsum('gts,bnsgc->bntgc', w, vc) + b_s.T[:, :, None]
    return u * s.reshape(B_, S_, SG_WIDTH)


def short_conv(xin, bg, cg, conv_w):
    z = cg * xin
    S_ = z.shape[1]
    zp = jnp.pad(z, ((0, 0), (CONV_K - 1, 0), (0, 0)))
    y = sum(conv_w[k] * zp[:, k:k + S_] for k in range(CONV_K))
    return bg * y


def multiscale_pool(p, w_pool, scale):
    B_, S_, _ = p.shape
    pg = p.reshape(B_, S_, len(POOL_WINDOWS), POOL_GROUP).astype(jnp.float32)
    cs = jnp.cumsum(pg, axis=1)
    t = jnp.arange(S_)
    outs = []
    for g, win in enumerate(POOL_WINDOWS):
        c = cs[:, :, g]
        lagged = jnp.pad(c, ((0, 0), (win, 0), (0, 0)))[:, :S_]
        cnt = jnp.minimum(t + 1, win).astype(jnp.float32)[None, :, None]
        outs.append((c - lagged) / cnt - pg[:, :, g])
    pooled = jnp.stack(outs, axis=2).astype(p.dtype)
    mixed = jnp.einsum('bsgc,gcd->bsgd', pooled, w_pool)
    return mixed.reshape(B_, S_, POOL_WIDTH) * scale


def setup_inputs(seed: int = 0) -> dict:
    key = jax.random.key(seed)
    ks = jax.random.split(key, 26)
    L = DEPTH

    def nrm(k, shape, fan_in):
        return jax.random.normal(k, shape, jnp.float32) * fan_in ** -0.5

    def gain(k, shape):
        return 1.0 + 0.05 * jax.random.normal(k, shape, jnp.float32)

    def small(k, shape, s):
        return s * jax.random.normal(k, shape, jnp.float32)

    x = jax.random.normal(ks[0], (BATCH, SEQ, D_MODEL), jnp.float32)
    positions = (jax.random.randint(ks[1], (BATCH, 1), 0, 4096, dtype=jnp.int32)
                 + jnp.arange(SEQ, dtype=jnp.int32)[None, :])
    return {
        'x': x,
        'positions': positions,
        'norm_mix_pre': gain(ks[2], (L, D_MODEL)),
        'w_in': nrm(ks[3], (L, D_MODEL, N_IN), D_MODEL),
        'gate_b': small(ks[4], (L, N_BRANCH * D_MODEL), 0.01),
        'q_norm': gain(ks[5], (L, Q_RANK)),
        'w_uq': nrm(ks[6], (L, Q_RANK, MLA_HEADS * (QK_NOPE_DIM + QK_ROPE_DIM)), Q_RANK),
        'kv_norm': gain(ks[7], (L, KV_RANK)),
        'w_ukv': nrm(ks[8], (L, KV_RANK, MLA_HEADS * (QK_NOPE_DIM + V_HEAD_DIM)), KV_RANK),
        'w_br_mla': nrm(ks[9], (L, MLA_HEADS * V_HEAD_DIM, D_MODEL), MLA_HEADS * V_HEAD_DIM),
        'sg_ln_g': gain(ks[10], (L, SG_WIDTH)),
        'sg_ln_b': small(ks[11], (L, SG_WIDTH), 0.02),
        'sg_w': nrm(ks[12], (L, SG_GROUPS, SG_CHUNK, SG_CHUNK), SG_CHUNK),
        'sg_b': gain(ks[13], (L, SG_GROUPS, SG_CHUNK)),
        'w_br_sg': nrm(ks[14], (L, SG_WIDTH, D_MODEL), SG_WIDTH),
        'conv_w': nrm(ks[15], (L, CONV_K, CONV_WIDTH), CONV_K),
        'w_br_conv': nrm(ks[16], (L, CONV_WIDTH, D_MODEL), CONV_WIDTH),
        'pool_w': nrm(ks[17], (L, len(POOL_WINDOWS), POOL_GROUP, POOL_GROUP), POOL_GROUP),
        'pool_scale': gain(ks[18], (L, POOL_WIDTH)),
        'w_br_pool': nrm(ks[19], (L, POOL_WIDTH, D_MODEL), POOL_WIDTH),
        'w_out': nrm(ks[20], (L, D_MODEL, D_MODEL), D_MODEL),
        'norm_mix_post': gain(ks[21], (L, D_MODEL)),
        'norm_ffn_pre': gain(ks[22], (L, D_MODEL)),
        'w_ff1': nrm(ks[23], (L, D_MODEL, D_FF), D_MODEL),
        'w_ff2': nrm(ks[24], (L, D_FF, D_MODEL), D_FF),
        'norm_ffn_post': gain(ks[25], (L, D_MODEL)),
    }


def reference(x, positions, norm_mix_pre, w_in, gate_b, q_norm, w_uq, kv_norm, w_ukv, w_br_mla,
              sg_ln_g, sg_ln_b, sg_w, sg_b, w_br_sg, conv_w, w_br_conv, pool_w, pool_scale, w_br_pool,
              w_out, norm_mix_post, norm_ffn_pre, w_ff1, w_ff2, norm_ffn_post):
    B_, S_, D_ = x.shape
    for l in range(DEPTH):
        h = rmsnorm(x, norm_mix_pre[l])
        (c_q, c_kv, k_r, sg_u, sg_v, cv_x, cv_b, cv_c, pool_in, gate_pre) = split_cols(h @ w_in[l])
        gates = jax.nn.sigmoid(gate_pre + gate_b[l]).reshape(B_, S_, N_BRANCH, D_)
        y_a = mla(c_q, c_kv, k_r, positions, q_norm[l], w_uq[l], kv_norm[l], w_ukv[l]) @ w_br_mla[l]
        y_b = spatial_gating(sg_u, sg_v, sg_ln_g[l], sg_ln_b[l], sg_w[l], sg_b[l]) @ w_br_sg[l]
        y_c = short_conv(cv_x, cv_b, cv_c, conv_w[l]) @ w_br_conv[l]
        y_d = multiscale_pool(pool_in, pool_w[l], pool_scale[l]) @ w_br_pool[l]
        merged = (gates[:, :, 0] * y_a + gates[:, :, 1] * y_b
                  + gates[:, :, 2] * y_c + gates[:, :, 3] * y_d)
        x = x + rmsnorm(merged @ w_out[l], norm_mix_post[l])
        h = rmsnorm(x, norm_ffn_pre[l])
        f = jnp.square(jax.nn.relu(h @ w_ff1[l])) @ w_ff2[l]
        x = x + rmsnorm(f, norm_ffn_post[l])
    return x
```

```python
import functools

import numpy as np
import jax
import jax.numpy as jnp
from jax import lax
from jax.experimental import pallas as pl
from jax.experimental.pallas import tpu as pltpu

D_MODEL = 1024
DEPTH = 4
HEADS = 4
NOPE = 64
ROPE = 32
VDIM = 64
Q_RANK = 256
KV_RANK = 128
ROPE_BASE = 10000.0
SG_WIDTH = 256
SG_GROUPS = 4
SG_CHUNK = 128
BR_WIDTH = 256
POOL_WINDOWS = (2, 4, 8, 16)
N_BRANCH = 4
D_FF = 4 * D_MODEL
EPS = 1e-6
HALO = 16
HEAD_LANES = 128
ATTN_COLS = Q_RANK + KV_RANK + ROPE
REST_COLS = 2 * SG_WIDTH + 3 * BR_WIDTH + BR_WIDTH + N_BRANCH * D_MODEL

T_ROPE = 1024
T_PROJ = 512
T_ATT = 512
T_MIX = 256
T_FFN = 512

VMEM_LIMIT = 56 * 1024 * 1024

BF16 = jnp.bfloat16
F32 = jnp.float32


def _dot(a, b):
    return jnp.dot(a, b, preferred_element_type=F32)


def _dot_nt(a, b):
    return lax.dot_general(a, b, (((1,), (1,)), ((), ())), preferred_element_type=F32)


def _rms(x, g):
    return x * lax.rsqrt(jnp.mean(x * x, axis=-1, keepdims=True) + EPS) * g


def _params():
    return pltpu.CompilerParams(dimension_semantics=("arbitrary",), vmem_limit_bytes=VMEM_LIMIT)


def _layer_spec(l, *tail):
    zeros = (0,) * len(tail)
    return pl.BlockSpec((None,) + tuple(tail), lambda *_: (l,) + zeros,
                        pipeline_mode=pl.Buffered(1))


def _row_spec(t, n):
    return pl.BlockSpec((t, n), lambda i: (i, 0))


def _rope_table_kernel(pos_ref, freq_ref, cos_ref, sin_ref):
    ang = pos_ref[...].astype(F32) * freq_ref[...]
    cos_ref[...] = jnp.cos(ang)
    sin_ref[...] = jnp.sin(ang)


def _rope_tables(positions):
    n = positions.size
    inv_freq = ROPE_BASE ** (-jnp.arange(0, ROPE, 2, dtype=F32) / ROPE)
    freq = jnp.zeros((HEAD_LANES,), F32)
    freq = freq.at[NOPE:NOPE + ROPE // 2].set(inv_freq).at[NOPE + ROPE // 2:NOPE + ROPE].set(inv_freq)
    return pl.pallas_call(
        _rope_table_kernel,
        grid=(n // T_ROPE,),
        in_specs=[pl.BlockSpec((T_ROPE, 1), lambda i: (i, 0)),
                  pl.BlockSpec((1, HEAD_LANES), lambda i: (0, 0))],
        out_specs=[_row_spec(T_ROPE, HEAD_LANES), _row_spec(T_ROPE, HEAD_LANES)],
        out_shape=[jax.ShapeDtypeStruct((n, HEAD_LANES), F32)] * 2,
        compiler_params=_params(),
        name="rope_tables",
    )(positions.reshape(n, 1), freq.reshape(1, HEAD_LANES))


def _attn_proj_kernel(x_ref, g_ref, wa_ref, qn_ref, wq_ref, kvn_ref, wkv_ref, cos_ref, sin_ref,
                      q_ref, k_ref, v_ref):
    h = _rms(x_ref[...], g_ref[...]).astype(BF16)
    pa = _dot(h, wa_ref[...])
    c_q = pa[:, :Q_RANK]
    c_kv = pa[:, Q_RANK:Q_RANK + KV_RANK]
    kr = pa[:, 384:512]
    kr_rot = pa[:, 512:640]
    cos = cos_ref[...]
    sin = sin_ref[...]
    cos4 = jnp.concatenate([cos] * HEADS, axis=1)
    sin4 = jnp.concatenate([sin] * HEADS, axis=1)

    qlr = _dot(_rms(c_q, qn_ref[...]).astype(BF16), wq_ref[...])
    nq = HEADS * HEAD_LANES
    scale = (NOPE + ROPE) ** -0.5
    q = (qlr[:, :nq] * cos4 + qlr[:, nq:] * sin4) * scale
    q_ref[...] = q.astype(BF16)

    kv = _dot(_rms(c_kv, kvn_ref[...]).astype(BF16), wkv_ref[...])
    k_rope = kr * cos + kr_rot * sin
    k = kv[:, :nq] + jnp.concatenate([k_rope] * HEADS, axis=1)
    k_ref[...] = k.astype(BF16)
    v_ref[...] = kv[:, nq:].astype(BF16)


def _attn_proj(l, x2, cos, sin, w):
    n = x2.shape[0]
    nq = HEADS * HEAD_LANES
    return pl.pallas_call(
        _attn_proj_kernel,
        grid=(n // T_PROJ,),
        in_specs=[_row_spec(T_PROJ, D_MODEL),
                  _layer_spec(l, 1, D_MODEL),
                  _layer_spec(l, D_MODEL, 640),
                  _layer_spec(l, 1, Q_RANK),
                  _layer_spec(l, Q_RANK, 2 * nq),
                  _layer_spec(l, 1, KV_RANK),
                  _layer_spec(l, KV_RANK, nq + HEADS * VDIM),
                  _row_spec(T_PROJ, HEAD_LANES),
                  _row_spec(T_PROJ, HEAD_LANES)],
        out_specs=[_row_spec(T_PROJ, nq), _row_spec(T_PROJ, nq), _row_spec(T_PROJ, HEADS * VDIM)],
        out_shape=[jax.ShapeDtypeStruct((n, nq), BF16),
                   jax.ShapeDtypeStruct((n, nq), BF16),
                   jax.ShapeDtypeStruct((n, HEADS * VDIM), BF16)],
        compiler_params=_params(),
        name="attn_proj",
    )(x2, w["norm_mix_pre"], w["w_attn_in"], w["q_norm"], w["w_q"], w["kv_norm"], w["w_kv"], cos, sin)


def _attn_kernel(q_ref, k_ref, v_ref, o_ref, m_sc, l_sc, acc_sc):
    i = pl.program_id(1)
    m_sc[...] = jnp.full(m_sc.shape, -1e30, F32)
    l_sc[...] = jnp.zeros(l_sc.shape, F32)
    acc_sc[...] = jnp.zeros(acc_sc.shape, F32)

    def step(j, masked):
        off = pl.multiple_of(j * T_ATT, T_ATT)
        for h in range(HEADS):
            q = q_ref[:, h * HEAD_LANES:(h + 1) * HEAD_LANES]
            k = k_ref[pl.ds(off, T_ATT), h * HEAD_LANES:(h + 1) * HEAD_LANES]
            v = v_ref[pl.ds(off, T_ATT), (h // 2) * 128:(h // 2 + 1) * 128]
            s = _dot_nt(q, k)
            if masked:
                row = lax.broadcasted_iota(jnp.int32, s.shape, 0)
                col = lax.broadcasted_iota(jnp.int32, s.shape, 1)
                s = jnp.where(col <= row, s, -jnp.inf)
            m_prev = m_sc[h]
            m_new = jnp.maximum(m_prev, jnp.max(s, axis=1, keepdims=True))
            alpha = jnp.exp(m_prev - m_new)
            p = jnp.exp(s - m_new)
            l_sc[h] = alpha * l_sc[h] + jnp.sum(p, axis=1, keepdims=True)
            acc_sc[h] = alpha * acc_sc[h] + _dot(p.astype(BF16), v)
            m_sc[h] = m_new

    def body(j, carry):
        step(j, False)
        return carry

    lax.fori_loop(0, i, body, 0)
    step(i, True)

    lane = lax.broadcasted_iota(jnp.int32, (T_ATT, 128), 1)
    for p in range(HEADS // 2):
        o_even = acc_sc[2 * p] / l_sc[2 * p]
        o_odd = acc_sc[2 * p + 1] / l_sc[2 * p + 1]
        o_ref[:, p * 128:(p + 1) * 128] = jnp.where(lane < VDIM, o_even, o_odd).astype(BF16)


def _attention(q, k, v, batch, seq):
    nq = seq // T_ATT
    cols = HEADS * HEAD_LANES
    return pl.pallas_call(
        _attn_kernel,
        grid=(batch, nq),
        in_specs=[pl.BlockSpec((T_ATT, cols), lambda b, i: (b * nq + i, 0)),
                  pl.BlockSpec((seq, cols), lambda b, i: (b, 0)),
                  pl.BlockSpec((seq, HEADS * VDIM), lambda b, i: (b, 0))],
        out_specs=pl.BlockSpec((T_ATT, HEADS * VDIM), lambda b, i: (b * nq + i, 0)),
        out_shape=jax.ShapeDtypeStruct((batch * seq, HEADS * VDIM), BF16),
        scratch_shapes=[pltpu.VMEM((HEADS, T_ATT, 1), F32),
                        pltpu.VMEM((HEADS, T_ATT, 1), F32),
                        pltpu.VMEM((HEADS, T_ATT, 128), F32)],
        compiler_params=pltpu.CompilerParams(dimension_semantics=("arbitrary", "arbitrary"),
                                             vmem_limit_bytes=VMEM_LIMIT),
        name="attention",
    )(q, k, v)


def _mixer_kernel(tiles_per_seq, x_ref, o_ref, g_ref, wb_ref, gb_ref, lng_ref, lnb_ref, wsg_ref, sgb_ref,
                  cw_ref, wpool_ref, pscale_ref, wbr_ref, wout_ref, gpost_ref,
                  out_ref, ez_ref, ep_ref):
    t = x_ref.shape[0]
    i = pl.program_id(0)
    seq_tile = lax.rem(i, tiles_per_seq)

    @pl.when(seq_tile == 0)
    def _():
        ez_ref[0:HALO, :] = jnp.zeros((HALO, BR_WIDTH), F32)
        ep_ref[0:HALO, :] = jnp.zeros((HALO, BR_WIDTH), F32)

    x = x_ref[...]
    h = _rms(x, g_ref[...]).astype(BF16)

    uv = _dot(h, wb_ref[:, 0:2 * SG_WIDTH])
    u = jax.nn.gelu(uv[:, :SG_WIDTH])
    gv = jax.nn.gelu(uv[:, SG_WIDTH:])
    mu = jnp.mean(gv, axis=-1, keepdims=True)
    gc = gv - mu
    vln = gc * lax.rsqrt(jnp.mean(gc * gc, axis=-1, keepdims=True) + EPS) * lng_ref[...] + lnb_ref[...]
    vb = vln.astype(BF16)
    trow = lax.broadcasted_iota(jnp.int32, (SG_CHUNK, SG_GROUPS * SG_CHUNK), 0)
    scol = lax.rem(lax.broadcasted_iota(jnp.int32, (SG_CHUNK, SG_GROUPS * SG_CHUNK), 1), SG_CHUNK)
    wsg = jnp.where(scol <= trow, wsg_ref[...], jnp.zeros((), BF16))
    lane_group = lax.broadcasted_iota(jnp.int32, (SG_CHUNK, SG_WIDTH), 1) // (SG_WIDTH // SG_GROUPS)
    mixed_chunks = []
    for c in range(t // SG_CHUNK):
        vc = vb[c * SG_CHUNK:(c + 1) * SG_CHUNK]
        rhs = jnp.concatenate([jnp.where(lane_group == g, vc, jnp.zeros((), BF16))
                               for g in range(SG_GROUPS)], axis=0)
        mixed_chunks.append(_dot(wsg, rhs) + sgb_ref[...])
    y_b = _dot((u * jnp.concatenate(mixed_chunks, axis=0)).astype(BF16), wbr_ref[1])

    o0 = 2 * SG_WIDTH
    cv = _dot(h, wb_ref[:, o0:o0 + 3 * BR_WIDTH])
    z = cv[:, 2 * BR_WIDTH:] * cv[:, :BR_WIDTH]
    ez_ref[HALO:HALO + t, :] = z
    conv = (cw_ref[0:1, :] * ez_ref[HALO - 2:HALO - 2 + t, :]
            + cw_ref[1:2, :] * ez_ref[HALO - 1:HALO - 1 + t, :]
            + cw_ref[2:3, :] * z)
    y_c = _dot((cv[:, BR_WIDTH:2 * BR_WIDTH] * conv).astype(BF16), wbr_ref[2])
    ez_ref[0:HALO, :] = ez_ref[t:t + HALO, :]

    o1 = o0 + 3 * BR_WIDTH
    ep_ref[HALO:HALO + t, :] = _dot(h, wb_ref[:, o1:o1 + BR_WIDTH])
    tpos = seq_tile * t + lax.broadcasted_iota(jnp.int32, (t, 128), 0) + 1
    lane_lo = lax.broadcasted_iota(jnp.int32, (t, 128), 1) < 64
    pooled = []
    for half, (w_lo, w_hi) in enumerate(((POOL_WINDOWS[0], POOL_WINDOWS[1]),
                                          (POOL_WINDOWS[2], POOL_WINDOWS[3]))):
        cols = slice(half * 128, (half + 1) * 128)
        cur = ep_ref[HALO:HALO + t, cols]
        run = cur
        for d in range(1, w_lo):
            run = run + ep_ref[HALO - d:HALO - d + t, cols]
        sum_lo = run
        for d in range(w_lo, w_hi):
            run = run + ep_ref[HALO - d:HALO - d + t, cols]
        win = jnp.where(lane_lo, w_lo, w_hi)
        cnt = jnp.minimum(tpos, win).astype(F32)
        pooled.append(jnp.where(lane_lo, sum_lo, run) / cnt - cur)
    pooled = jnp.concatenate(pooled, axis=1).astype(BF16)
    mixed_d = _dot(pooled, wpool_ref[...]) * pscale_ref[...]
    y_d = _dot(mixed_d.astype(BF16), wbr_ref[3])
    ep_ref[0:HALO, :] = ep_ref[t:t + HALO, :]

    y_a = _dot(o_ref[...], wbr_ref[0])
    o2 = o1 + BR_WIDTH
    merged = None
    for br, y in enumerate((y_a, y_b, y_c, y_d)):
        gate = jax.nn.sigmoid(_dot(h, wb_ref[:, o2 + br * D_MODEL:o2 + (br + 1) * D_MODEL])
                              + gb_ref[:, br * D_MODEL:(br + 1) * D_MODEL])
        merged = gate * y if merged is None else merged + gate * y
    mo = _dot(merged.astype(BF16), wout_ref[...])
    out_ref[...] = x + _rms(mo, gpost_ref[...])


def _mixer(l, x2, o, w, seq):
    n = x2.shape[0]
    t = T_MIX
    return pl.pallas_call(
        functools.partial(_mixer_kernel, seq // t),
        grid=(n // t,),
        in_specs=[_row_spec(t, D_MODEL),
                  _row_spec(t, HEADS * VDIM),
                  _layer_spec(l, 1, D_MODEL),
                  _layer_spec(l, D_MODEL, REST_COLS),
                  _layer_spec(l, 1, N_BRANCH * D_MODEL),
                  _layer_spec(l, 1, SG_WIDTH),
                  _layer_spec(l, 1, SG_WIDTH),
                  _layer_spec(l, SG_CHUNK, SG_GROUPS * SG_CHUNK),
                  _layer_spec(l, SG_CHUNK, SG_WIDTH),
                  _layer_spec(l, 3, BR_WIDTH),
                  _layer_spec(l, BR_WIDTH, BR_WIDTH),
                  _layer_spec(l, 1, BR_WIDTH),
                  _layer_spec(l, N_BRANCH, BR_WIDTH, D_MODEL),
                  _layer_spec(l, D_MODEL, D_MODEL),
                  _layer_spec(l, 1, D_MODEL)],
        out_specs=_row_spec(t, D_MODEL),
        out_shape=jax.ShapeDtypeStruct((n, D_MODEL), F32),
        scratch_shapes=[pltpu.VMEM((t + HALO, BR_WIDTH), F32),
                        pltpu.VMEM((t + HALO, BR_WIDTH), F32)],
        compiler_params=_params(),
        name="mixer",
    )(x2, o, w["norm_mix_pre"], w["w_rest_in"], w["gate_b"], w["sg_ln_g"], w["sg_ln_b"], w["w_sg"],
      w["sg_bias"], w["conv_w"], w["w_pool"], w["pool_scale"], w["w_br"], w["w_out"], w["norm_mix_post"])


def _ffn_kernel(x_ref, g_ref, w1_ref, w2_ref, gpost_ref, out_ref):
    x = x_ref[...]
    h = _rms(x, g_ref[...]).astype(BF16)
    a = jnp.maximum(_dot(h, w1_ref[...]), 0.0)
    f = _dot((a * a).astype(BF16), w2_ref[...])
    out_ref[...] = x + _rms(f, gpost_ref[...])


def _ffn(l, x2, w):
    n = x2.shape[0]
    return pl.pallas_call(
        _ffn_kernel,
        grid=(n // T_FFN,),
        in_specs=[_row_spec(T_FFN, D_MODEL),
                  _layer_spec(l, 1, D_MODEL),
                  _layer_spec(l, D_MODEL, D_FF),
                  _layer_spec(l, D_FF, D_MODEL),
                  _layer_spec(l, 1, D_MODEL)],
        out_specs=_row_spec(T_FFN, D_MODEL),
        out_shape=jax.ShapeDtypeStruct((n, D_MODEL), F32),
        compiler_params=_params(),
        name="ffn",
    )(x2, w["norm_ffn_pre"], w["w_ff1"], w["w_ff2"], w["norm_ffn_post"])


def _gather_cols(w, idx, sign):
    return jnp.take(w, jnp.asarray(idx), axis=-1) * jnp.asarray(sign, F32)


def _attn_in_layout():
    idx = np.zeros(640, np.int32)
    sign = np.zeros(640, np.float32)
    idx[:Q_RANK + KV_RANK] = np.arange(Q_RANK + KV_RANK)
    sign[:Q_RANK + KV_RANK] = 1.0
    r0 = Q_RANK + KV_RANK
    half = ROPE // 2
    for j in range(ROPE):
        idx[384 + NOPE + j] = r0 + j
        sign[384 + NOPE + j] = 1.0
    for j in range(half):
        idx[512 + NOPE + j] = r0 + half + j
        sign[512 + NOPE + j] = -1.0
        idx[512 + NOPE + half + j] = r0 + j
        sign[512 + NOPE + half + j] = 1.0
    return idx, sign


def _q_layout():
    nq = HEADS * HEAD_LANES
    idx = np.zeros(2 * nq, np.int32)
    sign = np.zeros(2 * nq, np.float32)
    half = ROPE // 2
    for h in range(HEADS):
        src = h * (NOPE + ROPE)
        dst = h * HEAD_LANES
        for j in range(NOPE + ROPE):
            idx[dst + j] = src + j
            sign[dst + j] = 1.0
        for j in range(half):
            idx[nq + dst + NOPE + j] = src + NOPE + half + j
            sign[nq + dst + NOPE + j] = -1.0
            idx[nq + dst + NOPE + half + j] = src + NOPE + j
            sign[nq + dst + NOPE + half + j] = 1.0
    return idx, sign


def _kv_layout():
    nq = HEADS * HEAD_LANES
    idx = np.zeros(nq + HEADS * VDIM, np.int32)
    sign = np.zeros(nq + HEADS * VDIM, np.float32)
    for h in range(HEADS):
        src = h * (NOPE + VDIM)
        for j in range(NOPE):
            idx[h * HEAD_LANES + j] = src + j
            sign[h * HEAD_LANES + j] = 1.0
        for j in range(VDIM):
            idx[nq + h * VDIM + j] = src + NOPE + j
            sign[nq + h * VDIM + j] = 1.0
    return idx, sign


def _prepare_weights(p):
    L = DEPTH
    row = lambda a: a.reshape(L, 1, a.shape[-1])
    eye = jnp.eye(len(POOL_WINDOWS), dtype=F32)
    group = BR_WIDTH // len(POOL_WINDOWS)
    return {
        "norm_mix_pre": row(p["norm_mix_pre"]),
        "w_attn_in": _gather_cols(p["w_in"], *_attn_in_layout()).astype(BF16),
        "q_norm": row(p["q_norm"]),
        "w_q": _gather_cols(p["w_uq"], *_q_layout()).astype(BF16),
        "kv_norm": row(p["kv_norm"]),
        "w_kv": _gather_cols(p["w_ukv"], *_kv_layout()).astype(BF16),
        "w_rest_in": p["w_in"][:, :, ATTN_COLS:].astype(BF16),
        "gate_b": row(p["gate_b"]),
        "sg_ln_g": row(p["sg_ln_g"]),
        "sg_ln_b": row(p["sg_ln_b"]),
        "w_sg": jnp.transpose(p["sg_w"], (0, 2, 1, 3)).reshape(L, SG_CHUNK, SG_GROUPS * SG_CHUNK).astype(BF16),
        "sg_bias": jnp.repeat(jnp.transpose(p["sg_b"], (0, 2, 1)), SG_WIDTH // SG_GROUPS, axis=-1),
        "conv_w": p["conv_w"],
        "w_pool": jnp.einsum("lgcd,gh->lgchd", p["pool_w"], eye).reshape(L, BR_WIDTH, BR_WIDTH).astype(BF16),
        "pool_scale": row(p["pool_scale"]),
        "w_br": jnp.stack([p["w_br_mla"], p["w_br_sg"], p["w_br_conv"], p["w_br_pool"]], axis=1).astype(BF16),
        "w_out": p["w_out"].astype(BF16),
        "norm_mix_post": row(p["norm_mix_post"]),
        "norm_ffn_pre": row(p["norm_ffn_pre"]),
        "w_ff1": p["w_ff1"].astype(BF16),
        "w_ff2": p["w_ff2"].astype(BF16),
        "norm_ffn_post": row(p["norm_ffn_post"]),
    }


def kernel(x, positions, norm_mix_pre, w_in, gate_b, q_norm, w_uq, kv_norm, w_ukv, w_br_mla, sg_ln_g, sg_ln_b, sg_w, sg_b, w_br_sg, conv_w, w_br_conv, pool_w, pool_scale, w_br_pool, w_out, norm_mix_post, norm_ffn_pre, w_ff1, w_ff2, norm_ffn_post):
    batch, seq, d = x.shape
    w = _prepare_weights(dict(
        norm_mix_pre=norm_mix_pre, w_in=w_in, gate_b=gate_b, q_norm=q_norm, w_uq=w_uq, kv_norm=kv_norm,
        w_ukv=w_ukv, w_br_mla=w_br_mla, sg_ln_g=sg_ln_g, sg_ln_b=sg_ln_b, sg_w=sg_w, sg_b=sg_b,
        w_br_sg=w_br_sg, conv_w=conv_w, w_br_conv=w_br_conv, pool_w=pool_w, pool_scale=pool_scale,
        w_br_pool=w_br_pool, w_out=w_out, norm_mix_post=norm_mix_post, norm_ffn_pre=norm_ffn_pre,
        w_ff1=w_ff1, w_ff2=w_ff2, norm_ffn_post=norm_ffn_post))
    cos, sin = _rope_tables(positions)
    x2 = x.reshape(batch * seq, d)
    for l in range(DEPTH):
        q, k, v = _attn_proj(l, x2, cos, sin, w)
        o = _attention(q, k, v, batch, seq)
        x2 = _mixer(l, x2, o, w, seq)
        x2 = _ffn(l, x2, w)
    return x2.reshape(batch, seq, d)
```

```python
import functools

import numpy as np
import jax
import jax.numpy as jnp
from jax import lax
from jax.experimental import pallas as pl
from jax.experimental.pallas import tpu as pltpu

D_MODEL = 1024
DEPTH = 4
HEADS = 4
NOPE = 64
ROPE = 32
VDIM = 64
Q_RANK = 256
KV_RANK = 128
ROPE_BASE = 10000.0
SG_WIDTH = 256
SG_GROUPS = 4
SG_CHUNK = 128
BR_WIDTH = 256
POOL_WINDOWS = (2, 4, 8, 16)
N_BRANCH = 4
D_FF = 4 * D_MODEL
EPS = 1e-6
HALO = 16
HEAD_LANES = 128
ATTN_COLS = Q_RANK + KV_RANK + ROPE
REST_COLS = 2 * SG_WIDTH + 3 * BR_WIDTH + BR_WIDTH + N_BRANCH * D_MODEL

V_ROWS = VDIM + 16
Q_SCALE = (NOPE + ROPE) ** -0.5 * 1.4426950408889634

T_ROPE = 1024
T_ATT = 512
T_MIX = 256
T_FFN = 512

VMEM_LIMIT = 56 * 1024 * 1024

BF16 = jnp.bfloat16
F32 = jnp.float32


def _dot(a, b):
    return jnp.dot(a, b, preferred_element_type=F32)


def _dot_nt(a, b):
    return lax.dot_general(a, b, (((1,), (1,)), ((), ())), preferred_element_type=F32)


def _rms(x, g):
    return x * lax.rsqrt(jnp.mean(x * x, axis=-1, keepdims=True) + EPS) * g


def _params():
    return pltpu.CompilerParams(dimension_semantics=("arbitrary",), vmem_limit_bytes=VMEM_LIMIT)


def _layer_spec(l, *tail):
    zeros = (0,) * len(tail)
    return pl.BlockSpec((None,) + tuple(tail), lambda *_: (l,) + zeros,
                        pipeline_mode=pl.Buffered(1))


def _row_spec(t, n):
    return pl.BlockSpec((t, n), lambda i: (i, 0))


def _rope_table_kernel(pos_ref, freq_ref, cos_ref, sin_ref, cos_t_ref, sin_t_ref):
    ang = pos_ref[...].astype(F32) * freq_ref[...]
    cos = jnp.cos(ang)
    sin = jnp.sin(ang)
    cos_ref[...] = cos
    sin_ref[...] = sin
    cos_t_ref[...] = cos.T
    sin_t_ref[...] = sin.T


def _rope_tables(positions):
    n = positions.size
    inv_freq = ROPE_BASE ** (-jnp.arange(0, ROPE, 2, dtype=F32) / ROPE)
    freq = jnp.zeros((HEAD_LANES,), F32)
    freq = freq.at[NOPE:NOPE + ROPE // 2].set(inv_freq).at[NOPE + ROPE // 2:NOPE + ROPE].set(inv_freq)
    return pl.pallas_call(
        _rope_table_kernel,
        grid=(n // T_ROPE,),
        in_specs=[pl.BlockSpec((T_ROPE, 1), lambda i: (i, 0)),
                  pl.BlockSpec((1, HEAD_LANES), lambda i: (0, 0))],
        out_specs=[_row_spec(T_ROPE, HEAD_LANES), _row_spec(T_ROPE, HEAD_LANES),
                   pl.BlockSpec((HEAD_LANES, T_ROPE), lambda i: (0, i)),
                   pl.BlockSpec((HEAD_LANES, T_ROPE), lambda i: (0, i))],
        out_shape=[jax.ShapeDtypeStruct((n, HEAD_LANES), F32)] * 2
        + [jax.ShapeDtypeStruct((HEAD_LANES, n), F32)] * 2,
        compiler_params=_params(),
        name="rope_tables",
    )(positions.reshape(n, 1), freq.reshape(1, HEAD_LANES))


def _attn_proj_kernel(x_ref, g_ref, wa_ref, qn_ref, wqt_ref, kvn_ref, wk_ref, wvt_ref,
                      cos_ref, sin_ref, cos_t_ref, sin_t_ref, qt_ref, k_ref, vt_ref):
    t = x_ref.shape[0]
    h = _rms(x_ref[...], g_ref[...]).astype(BF16)
    pa = _dot(h, wa_ref[...])
    c_q = pa[:, :Q_RANK]
    c_kv = pa[:, Q_RANK:Q_RANK + KV_RANK]
    kr = pa[:, 384:512]
    kr_rot = pa[:, 512:640]
    nq = HEADS * HEAD_LANES

    qlr_t = _dot_nt(wqt_ref[...], _rms(c_q, qn_ref[...]).astype(BF16))
    cos_t4 = jnp.concatenate([cos_t_ref[...]] * HEADS, axis=0)
    sin_t4 = jnp.concatenate([sin_t_ref[...]] * HEADS, axis=0)
    q_t = (qlr_t[:nq] * cos_t4 + qlr_t[nq:] * sin_t4) * Q_SCALE
    qt_ref[...] = q_t.astype(BF16)

    kvn = _rms(c_kv, kvn_ref[...]).astype(BF16)
    k_rope = kr * cos_ref[...] + kr_rot * sin_ref[...]
    k = _dot(kvn, wk_ref[...]) + jnp.concatenate([k_rope] * HEADS, axis=1)
    k_ref[...] = k.astype(BF16)

    v_t = _dot_nt(wvt_ref[...], kvn)
    ones_rows = (lax.broadcasted_iota(jnp.int32, (V_ROWS - VDIM, t), 0) == 0).astype(F32)
    pieces = []
    for hd in range(HEADS):
        pieces += [v_t[hd * VDIM:(hd + 1) * VDIM], ones_rows]
    vt_ref[...] = jnp.concatenate(pieces, axis=0).astype(BF16)


def _attn_proj(l, x2, tables, w):
    n = x2.shape[0]
    nq = HEADS * HEAD_LANES
    cos, sin, cos_t, sin_t = tables
    col_spec = pl.BlockSpec((HEAD_LANES, T_ATT), lambda i: (0, i))
    return pl.pallas_call(
        _attn_proj_kernel,
        grid=(n // T_ATT,),
        in_specs=[_row_spec(T_ATT, D_MODEL),
                  _layer_spec(l, 1, D_MODEL),
                  _layer_spec(l, D_MODEL, 640),
                  _layer_spec(l, 1, Q_RANK),
                  _layer_spec(l, 2 * nq, Q_RANK),
                  _layer_spec(l, 1, KV_RANK),
                  _layer_spec(l, KV_RANK, nq),
                  _layer_spec(l, HEADS * VDIM, KV_RANK),
                  _row_spec(T_ATT, HEAD_LANES),
                  _row_spec(T_ATT, HEAD_LANES),
                  col_spec, col_spec],
        out_specs=[pl.BlockSpec((None, nq, T_ATT), lambda i: (i, 0, 0)),
                   _row_spec(T_ATT, nq),
                   pl.BlockSpec((None, HEADS * V_ROWS, T_ATT), lambda i: (i, 0, 0))],
        out_shape=[jax.ShapeDtypeStruct((n // T_ATT, nq, T_ATT), BF16),
                   jax.ShapeDtypeStruct((n, nq), BF16),
                   jax.ShapeDtypeStruct((n // T_ATT, HEADS * V_ROWS, T_ATT), BF16)],
        compiler_params=_params(),
        name="attn_proj",
    )(x2, w["norm_mix_pre"], w["w_attn_in"], w["q_norm"], w["w_q_t"], w["kv_norm"], w["w_k"], w["w_v_t"],
      cos, sin, cos_t, sin_t)


def _attn_kernel(qt_ref, k_ref, vt_ref, o_ref, m_sc, acc_sc):
    i = pl.program_id(1)
    m_sc[...] = jnp.full(m_sc.shape, -1e30, F32)
    acc_sc[...] = jnp.zeros(acc_sc.shape, F32)

    def step(j, masked):
        off = pl.multiple_of(j * T_ATT, T_ATT)
        for h in range(HEADS):
            q_t = qt_ref[h * HEAD_LANES:(h + 1) * HEAD_LANES, :]
            k = k_ref[pl.ds(off, T_ATT), h * HEAD_LANES:(h + 1) * HEAD_LANES]
            s_t = _dot(k, q_t)
            if masked:
                key = lax.broadcasted_iota(jnp.int32, s_t.shape, 0)
                qry = lax.broadcasted_iota(jnp.int32, s_t.shape, 1)
                s_t = jnp.where(key <= qry, s_t, -jnp.inf)
            m_prev = m_sc[h]
            m_new = jnp.maximum(m_prev, jnp.max(s_t, axis=0, keepdims=True))
            alpha = jnp.exp2(m_prev - m_new)
            p_t = jnp.exp2(s_t - m_new).astype(BF16)
            v_t = vt_ref[j, h * V_ROWS:(h + 1) * V_ROWS, :]
            acc_sc[h] = alpha * acc_sc[h] + _dot(v_t, p_t)
            m_sc[h] = m_new

    def body(j, carry):
        step(j, False)
        return carry

    lax.fori_loop(0, i, body, 0)
    step(i, True)

    outs = []
    for h in range(HEADS):
        acc = acc_sc[h]
        outs.append(acc[:VDIM] / acc[VDIM:VDIM + 1])
    o_ref[...] = jnp.concatenate(outs, axis=0).T.astype(BF16)


def _attention(q_t, k, v_t, batch, seq):
    nq = seq // T_ATT
    cols = HEADS * HEAD_LANES
    return pl.pallas_call(
        _attn_kernel,
        grid=(batch, nq),
        in_specs=[pl.BlockSpec((None, cols, T_ATT), lambda b, i: (b * nq + i, 0, 0)),
                  pl.BlockSpec((seq, cols), lambda b, i: (b, 0)),
                  pl.BlockSpec((nq, HEADS * V_ROWS, T_ATT), lambda b, i: (b, 0, 0))],
        out_specs=pl.BlockSpec((T_ATT, HEADS * VDIM), lambda b, i: (b * nq + i, 0)),
        out_shape=jax.ShapeDtypeStruct((batch * seq, HEADS * VDIM), BF16),
        scratch_shapes=[pltpu.VMEM((HEADS, 1, T_ATT), F32),
                        pltpu.VMEM((HEADS, V_ROWS, T_ATT), F32)],
        compiler_params=pltpu.CompilerParams(dimension_semantics=("arbitrary", "arbitrary"),
                                             vmem_limit_bytes=VMEM_LIMIT),
        name="attention",
    )(q_t, k, v_t)


def _mixer_kernel(tiles_per_seq, x_ref, o_ref, g_ref, wb_ref, gb_ref, lng_ref, lnb_ref, wsg_ref, sgb_ref,
                  cw_ref, wpool_ref, pscale_ref, wbr_ref, wout_ref, gpost_ref,
                  out_ref, ez_ref, ep_ref):
    t = x_ref.shape[0]
    i = pl.program_id(0)
    seq_tile = lax.rem(i, tiles_per_seq)

    @pl.when(seq_tile == 0)
    def _():
        ez_ref[0:HALO, :] = jnp.zeros((HALO, BR_WIDTH), F32)
        ep_ref[0:HALO, :] = jnp.zeros((HALO, BR_WIDTH), F32)

    x = x_ref[...]
    h = _rms(x, g_ref[...]).astype(BF16)

    uv = _dot(h, wb_ref[:, 0:2 * SG_WIDTH])
    u = jax.nn.gelu(uv[:, :SG_WIDTH])
    gv = jax.nn.gelu(uv[:, SG_WIDTH:])
    mu = jnp.mean(gv, axis=-1, keepdims=True)
    gc = gv - mu
    vln = gc * lax.rsqrt(jnp.mean(gc * gc, axis=-1, keepdims=True) + EPS) * lng_ref[...] + lnb_ref[...]
    vb = vln.astype(BF16)
    trow = lax.broadcasted_iota(jnp.int32, (SG_CHUNK, SG_GROUPS * SG_CHUNK), 0)
    scol = lax.rem(lax.broadcasted_iota(jnp.int32, (SG_CHUNK, SG_GROUPS * SG_CHUNK), 1), SG_CHUNK)
    wsg = jnp.where(scol <= trow, wsg_ref[...], jnp.zeros((), BF16))
    lane_group = lax.broadcasted_iota(jnp.int32, (SG_CHUNK, SG_WIDTH), 1) // (SG_WIDTH // SG_GROUPS)
    mixed_chunks = []
    for c in range(t // SG_CHUNK):
        vc = vb[c * SG_CHUNK:(c + 1) * SG_CHUNK]
        rhs = jnp.concatenate([jnp.where(lane_group == g, vc, jnp.zeros((), BF16))
                               for g in range(SG_GROUPS)], axis=0)
        mixed_chunks.append(_dot(wsg, rhs) + sgb_ref[...])
    y_b = _dot((u * jnp.concatenate(mixed_chunks, axis=0)).astype(BF16), wbr_ref[1])

    o0 = 2 * SG_WIDTH
    cv = _dot(h, wb_ref[:, o0:o0 + 3 * BR_WIDTH])
    z = cv[:, 2 * BR_WIDTH:] * cv[:, :BR_WIDTH]
    ez_ref[HALO:HALO + t, :] = z
    conv = (cw_ref[0:1, :] * ez_ref[HALO - 2:HALO - 2 + t, :]
            + cw_ref[1:2, :] * ez_ref[HALO - 1:HALO - 1 + t, :]
            + cw_ref[2:3, :] * z)
    y_c = _dot((cv[:, BR_WIDTH:2 * BR_WIDTH] * conv).astype(BF16), wbr_ref[2])
    ez_ref[0:HALO, :] = ez_ref[t:t + HALO, :]

    o1 = o0 + 3 * BR_WIDTH
    ep_ref[HALO:HALO + t, :] = _dot(h, wb_ref[:, o1:o1 + BR_WIDTH])
    tpos = seq_tile * t + lax.broadcasted_iota(jnp.int32, (t, 128), 0) + 1
    lane_lo = lax.broadcasted_iota(jnp.int32, (t, 128), 1) < 64
    pooled = []
    for half, (w_lo, w_hi) in enumerate(((POOL_WINDOWS[0], POOL_WINDOWS[1]),
                                          (POOL_WINDOWS[2], POOL_WINDOWS[3]))):
        cols = slice(half * 128, (half + 1) * 128)
        cur = ep_ref[HALO:HALO + t, cols]
        run = cur
        for d in range(1, w_lo):
            run = run + ep_ref[HALO - d:HALO - d + t, cols]
        sum_lo = run
        for d in range(w_lo, w_hi):
            run = run + ep_ref[HALO - d:HALO - d + t, cols]
        win = jnp.where(lane_lo, w_lo, w_hi)
        cnt = jnp.minimum(tpos, win).astype(F32)
        pooled.append(jnp.where(lane_lo, sum_lo, run) / cnt - cur)
    pooled = jnp.concatenate(pooled, axis=1).astype(BF16)
    mixed_d = _dot(pooled, wpool_ref[...]) * pscale_ref[...]
    y_d = _dot(mixed_d.astype(BF16), wbr_ref[3])
    ep_ref[0:HALO, :] = ep_ref[t:t + HALO, :]

    y_a = _dot(o_ref[...], wbr_ref[0])
    o2 = o1 + BR_WIDTH
    merged = None
    for br, y in enumerate((y_a, y_b, y_c, y_d)):
        gate = jax.nn.sigmoid(_dot(h, wb_ref[:, o2 + br * D_MODEL:o2 + (br + 1) * D_MODEL])
                              + gb_ref[:, br * D_MODEL:(br + 1) * D_MODEL])
        merged = gate * y if merged is None else merged + gate * y
    mo = _dot(merged.astype(BF16), wout_ref[...])
    out_ref[...] = x + _rms(mo, gpost_ref[...])


def _mixer(l, x2, o, w, seq):
    n = x2.shape[0]
    t = T_MIX
    return pl.pallas_call(
        functools.partial(_mixer_kernel, seq // t),
        grid=(n // t,),
        in_specs=[_row_spec(t, D_MODEL),
                  _row_spec(t, HEADS * VDIM),
                  _layer_spec(l, 1, D_MODEL),
                  _layer_spec(l, D_MODEL, REST_COLS),
                  _layer_spec(l, 1, N_BRANCH * D_MODEL),
                  _layer_spec(l, 1, SG_WIDTH),
                  _layer_spec(l, 1, SG_WIDTH),
                  _layer_spec(l, SG_CHUNK, SG_GROUPS * SG_CHUNK),
                  _layer_spec(l, SG_CHUNK, SG_WIDTH),
                  _layer_spec(l, 3, BR_WIDTH),
                  _layer_spec(l, BR_WIDTH, BR_WIDTH),
                  _layer_spec(l, 1, BR_WIDTH),
                  _layer_spec(l, N_BRANCH, BR_WIDTH, D_MODEL),
                  _layer_spec(l, D_MODEL, D_MODEL),
                  _layer_spec(l, 1, D_MODEL)],
        out_specs=_row_spec(t, D_MODEL),
        out_shape=jax.ShapeDtypeStruct((n, D_MODEL), F32),
        scratch_shapes=[pltpu.VMEM((t + HALO, BR_WIDTH), F32),
                        pltpu.VMEM((t + HALO, BR_WIDTH), F32)],
        compiler_params=_params(),
        name="mixer",
    )(x2, o, w["norm_mix_pre"], w["w_rest_in"], w["gate_b"], w["sg_ln_g"], w["sg_ln_b"], w["w_sg"],
      w["sg_bias"], w["conv_w"], w["w_pool"], w["pool_scale"], w["w_br"], w["w_out"], w["norm_mix_post"])


def _ffn_kernel(x_ref, g_ref, w1_ref, w2_ref, gpost_ref, out_ref):
    x = x_ref[...]
    h = _rms(x, g_ref[...]).astype(BF16)
    a = jnp.maximum(_dot(h, w1_ref[...]), 0.0)
    f = _dot((a * a).astype(BF16), w2_ref[...])
    out_ref[...] = x + _rms(f, gpost_ref[...])


def _ffn(l, x2, w):
    n = x2.shape[0]
    return pl.pallas_call(
        _ffn_kernel,
        grid=(n // T_FFN,),
        in_specs=[_row_spec(T_FFN, D_MODEL),
                  _layer_spec(l, 1, D_MODEL),
                  _layer_spec(l, D_MODEL, D_FF),
                  _layer_spec(l, D_FF, D_MODEL),
                  _layer_spec(l, 1, D_MODEL)],
        out_specs=_row_spec(T_FFN, D_MODEL),
        out_shape=jax.ShapeDtypeStruct((n, D_MODEL), F32),
        compiler_params=_params(),
        name="ffn",
    )(x2, w["norm_ffn_pre"], w["w_ff1"], w["w_ff2"], w["norm_ffn_post"])


def _gather_cols(w, idx, sign):
    return jnp.take(w, jnp.asarray(idx), axis=-1) * jnp.asarray(sign, F32)


def _attn_in_layout():
    idx = np.zeros(640, np.int32)
    sign = np.zeros(640, np.float32)
    idx[:Q_RANK + KV_RANK] = np.arange(Q_RANK + KV_RANK)
    sign[:Q_RANK + KV_RANK] = 1.0
    r0 = Q_RANK + KV_RANK
    half = ROPE // 2
    for j in range(ROPE):
        idx[384 + NOPE + j] = r0 + j
        sign[384 + NOPE + j] = 1.0
    for j in range(half):
        idx[512 + NOPE + j] = r0 + half + j
        sign[512 + NOPE + j] = -1.0
        idx[512 + NOPE + half + j] = r0 + j
        sign[512 + NOPE + half + j] = 1.0
    return idx, sign


def _q_layout():
    nq = HEADS * HEAD_LANES
    idx = np.zeros(2 * nq, np.int32)
    sign = np.zeros(2 * nq, np.float32)
    half = ROPE // 2
    for h in range(HEADS):
        src = h * (NOPE + ROPE)
        dst = h * HEAD_LANES
        for j in range(NOPE + ROPE):
            idx[dst + j] = src + j
            sign[dst + j] = 1.0
        for j in range(half):
            idx[nq + dst + NOPE + j] = src + NOPE + half + j
            sign[nq + dst + NOPE + j] = -1.0
            idx[nq + dst + NOPE + half + j] = src + NOPE + j
            sign[nq + dst + NOPE + half + j] = 1.0
    return idx, sign


def _k_layout():
    nq = HEADS * HEAD_LANES
    idx = np.zeros(nq, np.int32)
    sign = np.zeros(nq, np.float32)
    for h in range(HEADS):
        src = h * (NOPE + VDIM)
        for j in range(NOPE):
            idx[h * HEAD_LANES + j] = src + j
            sign[h * HEAD_LANES + j] = 1.0
    return idx, sign


def _v_layout():
    idx = np.zeros(HEADS * VDIM, np.int32)
    for h in range(HEADS):
        for j in range(VDIM):
            idx[h * VDIM + j] = h * (NOPE + VDIM) + NOPE + j
    return idx, np.ones(HEADS * VDIM, np.float32)


def _prepare_weights(p):
    L = DEPTH
    row = lambda a: a.reshape(L, 1, a.shape[-1])
    eye = jnp.eye(len(POOL_WINDOWS), dtype=F32)
    group = BR_WIDTH // len(POOL_WINDOWS)
    return {
        "norm_mix_pre": row(p["norm_mix_pre"]),
        "w_attn_in": _gather_cols(p["w_in"], *_attn_in_layout()).astype(BF16),
        "q_norm": row(p["q_norm"]),
        "w_q_t": jnp.swapaxes(_gather_cols(p["w_uq"], *_q_layout()), 1, 2).astype(BF16),
        "kv_norm": row(p["kv_norm"]),
        "w_k": _gather_cols(p["w_ukv"], *_k_layout()).astype(BF16),
        "w_v_t": jnp.swapaxes(_gather_cols(p["w_ukv"], *_v_layout()), 1, 2).astype(BF16),
        "w_rest_in": p["w_in"][:, :, ATTN_COLS:].astype(BF16),
        "gate_b": row(p["gate_b"]),
        "sg_ln_g": row(p["sg_ln_g"]),
        "sg_ln_b": row(p["sg_ln_b"]),
        "w_sg": jnp.transpose(p["sg_w"], (0, 2, 1, 3)).reshape(L, SG_CHUNK, SG_GROUPS * SG_CHUNK).astype(BF16),
        "sg_bias": jnp.repeat(jnp.transpose(p["sg_b"], (0, 2, 1)), SG_WIDTH // SG_GROUPS, axis=-1),
        "conv_w": p["conv_w"],
        "w_pool": jnp.einsum("lgcd,gh->lgchd", p["pool_w"], eye).reshape(L, BR_WIDTH, BR_WIDTH).astype(BF16),
        "pool_scale": row(p["pool_scale"]),
        "w_br": jnp.stack([p["w_br_mla"], p["w_br_sg"], p["w_br_conv"], p["w_br_pool"]], axis=1).astype(BF16),
        "w_out": p["w_out"].astype(BF16),
        "norm_mix_post": row(p["norm_mix_post"]),
        "norm_ffn_pre": row(p["norm_ffn_pre"]),
        "w_ff1": p["w_ff1"].astype(BF16),
        "w_ff2": p["w_ff2"].astype(BF16),
        "norm_ffn_post": row(p["norm_ffn_post"]),
    }


def kernel(x, positions, norm_mix_pre, w_in, gate_b, q_norm, w_uq, kv_norm, w_ukv, w_br_mla, sg_ln_g, sg_ln_b, sg_w, sg_b, w_br_sg, conv_w, w_br_conv, pool_w, pool_scale, w_br_pool, w_out, norm_mix_post, norm_ffn_pre, w_ff1, w_ff2, norm_ffn_post):
    batch, seq, d = x.shape
    w = _prepare_weights(dict(
        norm_mix_pre=norm_mix_pre, w_in=w_in, gate_b=gate_b, q_norm=q_norm, w_uq=w_uq, kv_norm=kv_norm,
        w_ukv=w_ukv, w_br_mla=w_br_mla, sg_ln_g=sg_ln_g, sg_ln_b=sg_ln_b, sg_w=sg_w, sg_b=sg_b,
        w_br_sg=w_br_sg, conv_w=conv_w, w_br_conv=w_br_conv, pool_w=pool_w, pool_scale=pool_scale,
        w_br_pool=w_br_pool, w_out=w_out, norm_mix_post=norm_mix_post, norm_ffn_pre=norm_ffn_pre,
        w_ff1=w_ff1, w_ff2=w_ff2, norm_ffn_post=norm_ffn_post))
    tables = _rope_tables(positions)
    x2 = x.reshape(batch * seq, d)
    for l in range(DEPTH):
        q_t, k, v_t = _attn_proj(l, x2, tables, w)
        o = _attention(q_t, k, v_t, batch, seq)
        x2 = _mixer(l, x2, o, w, seq)
        x2 = _ffn(l, x2, w)
    return x2.reshape(batch, seq, d)
```

```python
import functools

import numpy as np
import jax
import jax.numpy as jnp
from jax import lax
from jax.experimental import pallas as pl
from jax.experimental.pallas import tpu as pltpu

D_MODEL = 1024
DEPTH = 4
HEADS = 4
NOPE = 64
ROPE = 32
VDIM = 64
Q_RANK = 256
KV_RANK = 128
ROPE_BASE = 10000.0
SG_WIDTH = 256
SG_GROUPS = 4
SG_CHUNK = 128
BR_WIDTH = 256
POOL_WINDOWS = (2, 4, 8, 16)
N_BRANCH = 4
D_FF = 4 * D_MODEL
EPS = 1e-6
HALO = 16
HEAD_LANES = 128
ATTN_COLS = Q_RANK + KV_RANK + ROPE
REST_COLS = 2 * SG_WIDTH + 3 * BR_WIDTH + BR_WIDTH + N_BRANCH * D_MODEL

V_ROWS = VDIM + 16
Q_SCALE = (NOPE + ROPE) ** -0.5 * 1.4426950408889634

T_ROPE = 1024
T_ATT = 512
T_MIX = 512
T_FFN = 512

VMEM_LIMIT = 56 * 1024 * 1024

BF16 = jnp.bfloat16
F32 = jnp.float32


def _dot(a, b):
    return jnp.dot(a, b, preferred_element_type=F32)


def _dot_nt(a, b):
    return lax.dot_general(a, b, (((1,), (1,)), ((), ())), preferred_element_type=F32)


def _rms(x, g):
    return x * lax.rsqrt(jnp.mean(x * x, axis=-1, keepdims=True) + EPS) * g


def _params():
    return pltpu.CompilerParams(dimension_semantics=("arbitrary",), vmem_limit_bytes=VMEM_LIMIT)


def _layer_spec(l, *tail):
    zeros = (0,) * len(tail)
    return pl.BlockSpec((None,) + tuple(tail), lambda *_: (l,) + zeros,
                        pipeline_mode=pl.Buffered(1))


def _row_spec(t, n):
    return pl.BlockSpec((t, n), lambda i: (i, 0))


def _rope_table_kernel(pos_ref, freq_ref, cos_ref, sin_ref, cos_t_ref, sin_t_ref):
    ang = pos_ref[...].astype(F32) * freq_ref[...]
    cos = jnp.cos(ang)
    sin = jnp.sin(ang)
    cos_ref[...] = cos
    sin_ref[...] = sin
    cos_t_ref[...] = cos.T
    sin_t_ref[...] = sin.T


def _rope_tables(positions):
    n = positions.size
    inv_freq = ROPE_BASE ** (-jnp.arange(0, ROPE, 2, dtype=F32) / ROPE)
    freq = jnp.zeros((HEAD_LANES,), F32)
    freq = freq.at[NOPE:NOPE + ROPE // 2].set(inv_freq).at[NOPE + ROPE // 2:NOPE + ROPE].set(inv_freq)
    return pl.pallas_call(
        _rope_table_kernel,
        grid=(n // T_ROPE,),
        in_specs=[pl.BlockSpec((T_ROPE, 1), lambda i: (i, 0)),
                  pl.BlockSpec((1, HEAD_LANES), lambda i: (0, 0))],
        out_specs=[_row_spec(T_ROPE, HEAD_LANES), _row_spec(T_ROPE, HEAD_LANES),
                   pl.BlockSpec((HEAD_LANES, T_ROPE), lambda i: (0, i)),
                   pl.BlockSpec((HEAD_LANES, T_ROPE), lambda i: (0, i))],
        out_shape=[jax.ShapeDtypeStruct((n, HEAD_LANES), F32)] * 2
        + [jax.ShapeDtypeStruct((HEAD_LANES, n), F32)] * 2,
        compiler_params=_params(),
        name="rope_tables",
    )(positions.reshape(n, 1), freq.reshape(1, HEAD_LANES))


def _attn_proj_kernel(x_ref, g_ref, wa_ref, qn_ref, wqt_ref, kvn_ref, wk_ref, wvt_ref,
                      cos_ref, sin_ref, cos_t_ref, sin_t_ref, qt_ref, k_ref, vt_ref):
    t = x_ref.shape[0]
    h = _rms(x_ref[...], g_ref[...]).astype(BF16)
    pa = _dot(h, wa_ref[...])
    c_q = pa[:, :Q_RANK]
    c_kv = pa[:, Q_RANK:Q_RANK + KV_RANK]
    kr = pa[:, 384:512]
    kr_rot = pa[:, 512:640]
    nq = HEADS * HEAD_LANES

    qlr_t = _dot_nt(wqt_ref[...], _rms(c_q, qn_ref[...]).astype(BF16))
    cos_t4 = jnp.concatenate([cos_t_ref[...]] * HEADS, axis=0)
    sin_t4 = jnp.concatenate([sin_t_ref[...]] * HEADS, axis=0)
    q_t = (qlr_t[:nq] * cos_t4 + qlr_t[nq:] * sin_t4) * Q_SCALE
    qt_ref[...] = q_t.astype(BF16)

    kvn = _rms(c_kv, kvn_ref[...]).astype(BF16)
    k_rope = kr * cos_ref[...] + kr_rot * sin_ref[...]
    k = _dot(kvn, wk_ref[...]) + jnp.concatenate([k_rope] * HEADS, axis=1)
    k_ref[...] = k.astype(BF16)

    v_t = _dot_nt(wvt_ref[...], kvn)
    ones_rows = (lax.broadcasted_iota(jnp.int32, (V_ROWS - VDIM, t), 0) == 0).astype(F32)
    pieces = []
    for hd in range(HEADS):
        pieces += [v_t[hd * VDIM:(hd + 1) * VDIM], ones_rows]
    vt_ref[...] = jnp.concatenate(pieces, axis=0).astype(BF16)


def _attn_proj(l, x2, tables, w):
    n = x2.shape[0]
    nq = HEADS * HEAD_LANES
    cos, sin, cos_t, sin_t = tables
    col_spec = pl.BlockSpec((HEAD_LANES, T_ATT), lambda i: (0, i))
    return pl.pallas_call(
        _attn_proj_kernel,
        grid=(n // T_ATT,),
        in_specs=[_row_spec(T_ATT, D_MODEL),
                  _layer_spec(l, 1, D_MODEL),
                  _layer_spec(l, D_MODEL, 640),
                  _layer_spec(l, 1, Q_RANK),
                  _layer_spec(l, 2 * nq, Q_RANK),
                  _layer_spec(l, 1, KV_RANK),
                  _layer_spec(l, KV_RANK, nq),
                  _layer_spec(l, HEADS * VDIM, KV_RANK),
                  _row_spec(T_ATT, HEAD_LANES),
                  _row_spec(T_ATT, HEAD_LANES),
                  col_spec, col_spec],
        out_specs=[pl.BlockSpec((None, nq, T_ATT), lambda i: (i, 0, 0)),
                   _row_spec(T_ATT, nq),
                   pl.BlockSpec((None, HEADS * V_ROWS, T_ATT), lambda i: (i, 0, 0))],
        out_shape=[jax.ShapeDtypeStruct((n // T_ATT, nq, T_ATT), BF16),
                   jax.ShapeDtypeStruct((n, nq), BF16),
                   jax.ShapeDtypeStruct((n // T_ATT, HEADS * V_ROWS, T_ATT), BF16)],
        compiler_params=_params(),
        name="attn_proj",
    )(x2, w["norm_mix_pre"], w["w_attn_in"], w["q_norm"], w["w_q_t"], w["kv_norm"], w["w_k"], w["w_v_t"],
      cos, sin, cos_t, sin_t)


def _attn_kernel(qt_ref, k_ref, vt_ref, o_ref, m_sc, acc_sc, sa_sc, sb_sc):
    i = pl.program_id(1)
    m_sc[...] = jnp.full(m_sc.shape, -1e30, F32)
    acc_sc[...] = jnp.zeros(acc_sc.shape, F32)

    def scores(j, h):
        off = pl.multiple_of(j * T_ATT, T_ATT)
        k = k_ref[pl.ds(off, T_ATT), h * HEAD_LANES:(h + 1) * HEAD_LANES]
        return _dot(k, qt_ref[h * HEAD_LANES:(h + 1) * HEAD_LANES, :])

    def accumulate(j, h, s_t, masked):
        if masked:
            key = lax.broadcasted_iota(jnp.int32, s_t.shape, 0)
            qry = lax.broadcasted_iota(jnp.int32, s_t.shape, 1)
            s_t = jnp.where(key <= qry, s_t, -jnp.inf)
        m_prev = m_sc[h]
        m_new = jnp.maximum(m_prev, jnp.max(s_t, axis=0, keepdims=True))
        alpha = jnp.exp2(m_prev - m_new)
        p_t = jnp.exp2(s_t - m_new).astype(BF16)
        v_t = vt_ref[j, h * V_ROWS:(h + 1) * V_ROWS, :]
        acc_sc[h] = alpha * acc_sc[h] + _dot(v_t, p_t)
        m_sc[h] = m_new

    def step(j, cur, nxt):
        for h in range(HEADS):
            if nxt is not None:
                nxt[h] = scores(j + 1, h)
            accumulate(j, h, cur[h], nxt is None)

    def pair(jj, carry):
        step(2 * jj, sa_sc, sb_sc)
        step(2 * jj + 1, sb_sc, sa_sc)
        return carry

    for h in range(HEADS):
        sa_sc[h] = scores(0, h)
    lax.fori_loop(0, i // 2, pair, 0)

    @pl.when(i % 2 == 0)
    def _():
        step(i, sa_sc, None)

    @pl.when(i % 2 == 1)
    def _():
        step(i - 1, sa_sc, sb_sc)
        step(i, sb_sc, None)

    outs = []
    for h in range(HEADS):
        acc = acc_sc[h]
        outs.append(acc[:VDIM] / acc[VDIM:VDIM + 1])
    o_ref[...] = jnp.concatenate(outs, axis=0).T.astype(BF16)


def _attention(q_t, k, v_t, batch, seq):
    nq = seq // T_ATT
    cols = HEADS * HEAD_LANES
    return pl.pallas_call(
        _attn_kernel,
        grid=(batch, nq),
        in_specs=[pl.BlockSpec((None, cols, T_ATT), lambda b, i: (b * nq + i, 0, 0)),
                  pl.BlockSpec((seq, cols), lambda b, i: (b, 0)),
                  pl.BlockSpec((nq, HEADS * V_ROWS, T_ATT), lambda b, i: (b, 0, 0))],
        out_specs=pl.BlockSpec((T_ATT, HEADS * VDIM), lambda b, i: (b * nq + i, 0)),
        out_shape=jax.ShapeDtypeStruct((batch * seq, HEADS * VDIM), BF16),
        scratch_shapes=[pltpu.VMEM((HEADS, 1, T_ATT), F32),
                        pltpu.VMEM((HEADS, V_ROWS, T_ATT), F32),
                        pltpu.VMEM((HEADS, T_ATT, T_ATT), F32),
                        pltpu.VMEM((HEADS, T_ATT, T_ATT), F32)],
        compiler_params=pltpu.CompilerParams(dimension_semantics=("arbitrary", "arbitrary"),
                                             vmem_limit_bytes=VMEM_LIMIT),
        name="attention",
    )(q_t, k, v_t)


def _mixer_kernel(tiles_per_seq, x_ref, o_ref, g_ref, wb_ref, gb_ref, lng_ref, lnb_ref, wsg_ref, sgb_ref,
                  cw_ref, wpool_ref, pscale_ref, wbr_ref, wout_ref, gpost_ref,
                  out_ref, ez_ref, ep_ref):
    t = x_ref.shape[0]
    i = pl.program_id(0)
    seq_tile = lax.rem(i, tiles_per_seq)

    @pl.when(seq_tile == 0)
    def _():
        ez_ref[0:HALO, :] = jnp.zeros((HALO, BR_WIDTH), F32)
        ep_ref[0:HALO, :] = jnp.zeros((HALO, BR_WIDTH), F32)

    x = x_ref[...]
    h = _rms(x, g_ref[...]).astype(BF16)
    o0 = 2 * SG_WIDTH
    o1 = o0 + 3 * BR_WIDTH
    o2 = o1 + BR_WIDTH

    def gate_pre(br):
        return (_dot(h, wb_ref[:, o2 + br * D_MODEL:o2 + (br + 1) * D_MODEL])
                + gb_ref[:, br * D_MODEL:(br + 1) * D_MODEL])

    uv = _dot(h, wb_ref[:, 0:2 * SG_WIDTH])
    gp_b = gate_pre(1)
    u = jax.nn.gelu(uv[:, :SG_WIDTH])
    gv = jax.nn.gelu(uv[:, SG_WIDTH:])
    mu = jnp.mean(gv, axis=-1, keepdims=True)
    gc = gv - mu
    vln = gc * lax.rsqrt(jnp.mean(gc * gc, axis=-1, keepdims=True) + EPS) * lng_ref[...] + lnb_ref[...]
    vb = vln.astype(BF16)
    trow = lax.broadcasted_iota(jnp.int32, (SG_CHUNK, SG_GROUPS * SG_CHUNK), 0)
    scol = lax.rem(lax.broadcasted_iota(jnp.int32, (SG_CHUNK, SG_GROUPS * SG_CHUNK), 1), SG_CHUNK)
    wsg = jnp.where(scol <= trow, wsg_ref[...], jnp.zeros((), BF16))
    lane_group = lax.broadcasted_iota(jnp.int32, (SG_CHUNK, SG_WIDTH), 1) // (SG_WIDTH // SG_GROUPS)
    mixed_chunks = []
    for c in range(t // SG_CHUNK):
        vc = vb[c * SG_CHUNK:(c + 1) * SG_CHUNK]
        rhs = jnp.concatenate([jnp.where(lane_group == g, vc, jnp.zeros((), BF16))
                               for g in range(SG_GROUPS)], axis=0)
        mixed_chunks.append(_dot(wsg, rhs) + sgb_ref[...])

    cv = _dot(h, wb_ref[:, o0:o0 + 3 * BR_WIDTH])
    gp_c = gate_pre(2)
    y_b = _dot((u * jnp.concatenate(mixed_chunks, axis=0)).astype(BF16), wbr_ref[1])
    merged = jax.nn.sigmoid(gp_b) * y_b
    z = cv[:, 2 * BR_WIDTH:] * cv[:, :BR_WIDTH]
    ez_ref[HALO:HALO + t, :] = z
    conv = (cw_ref[0:1, :] * ez_ref[HALO - 2:HALO - 2 + t, :]
            + cw_ref[1:2, :] * ez_ref[HALO - 1:HALO - 1 + t, :]
            + cw_ref[2:3, :] * z)
    ez_ref[0:HALO, :] = ez_ref[t:t + HALO, :]

    ep_ref[HALO:HALO + t, :] = _dot(h, wb_ref[:, o1:o1 + BR_WIDTH])
    gp_d = gate_pre(3)
    y_c = _dot((cv[:, BR_WIDTH:2 * BR_WIDTH] * conv).astype(BF16), wbr_ref[2])
    merged = merged + jax.nn.sigmoid(gp_c) * y_c
    tpos = seq_tile * t + lax.broadcasted_iota(jnp.int32, (t, 128), 0) + 1
    lane_lo = lax.broadcasted_iota(jnp.int32, (t, 128), 1) < 64
    pooled = []
    for half, (w_lo, w_hi) in enumerate(((POOL_WINDOWS[0], POOL_WINDOWS[1]),
                                          (POOL_WINDOWS[2], POOL_WINDOWS[3]))):
        cols = slice(half * 128, (half + 1) * 128)
        cur = ep_ref[HALO:HALO + t, cols]
        run = cur
        for d in range(1, w_lo):
            run = run + ep_ref[HALO - d:HALO - d + t, cols]
        sum_lo = run
        for d in range(w_lo, w_hi):
            run = run + ep_ref[HALO - d:HALO - d + t, cols]
        win = jnp.where(lane_lo, w_lo, w_hi)
        cnt = jnp.minimum(tpos, win).astype(F32)
        pooled.append(jnp.where(lane_lo, sum_lo, run) / cnt - cur)
    pooled = jnp.concatenate(pooled, axis=1).astype(BF16)
    ep_ref[0:HALO, :] = ep_ref[t:t + HALO, :]
    mixed_d = _dot(pooled, wpool_ref[...]) * pscale_ref[...]
    gp_a = gate_pre(0)
    y_d = _dot(mixed_d.astype(BF16), wbr_ref[3])
    merged = merged + jax.nn.sigmoid(gp_d) * y_d

    y_a = _dot(o_ref[...], wbr_ref[0])
    half = t // 2
    for r in range(2):
        rows = slice(r * half, (r + 1) * half)
        m_r = merged[rows] + jax.nn.sigmoid(gp_a[rows]) * y_a[rows]
        mo = _dot(m_r.astype(BF16), wout_ref[...])
        out_ref[rows, :] = x[rows] + _rms(mo, gpost_ref[...])


def _mixer(l, x2, o, w, seq):
    n = x2.shape[0]
    t = T_MIX
    return pl.pallas_call(
        functools.partial(_mixer_kernel, seq // t),
        grid=(n // t,),
        in_specs=[_row_spec(t, D_MODEL),
                  _row_spec(t, HEADS * VDIM),
                  _layer_spec(l, 1, D_MODEL),
                  _layer_spec(l, D_MODEL, REST_COLS),
                  _layer_spec(l, 1, N_BRANCH * D_MODEL),
                  _layer_spec(l, 1, SG_WIDTH),
                  _layer_spec(l, 1, SG_WIDTH),
                  _layer_spec(l, SG_CHUNK, SG_GROUPS * SG_CHUNK),
                  _layer_spec(l, SG_CHUNK, SG_WIDTH),
                  _layer_spec(l, 3, BR_WIDTH),
                  _layer_spec(l, BR_WIDTH, BR_WIDTH),
                  _layer_spec(l, 1, BR_WIDTH),
                  _layer_spec(l, N_BRANCH, BR_WIDTH, D_MODEL),
                  _layer_spec(l, D_MODEL, D_MODEL),
                  _layer_spec(l, 1, D_MODEL)],
        out_specs=_row_spec(t, D_MODEL),
        out_shape=jax.ShapeDtypeStruct((n, D_MODEL), F32),
        scratch_shapes=[pltpu.VMEM((t + HALO, BR_WIDTH), F32),
                        pltpu.VMEM((t + HALO, BR_WIDTH), F32)],
        compiler_params=_params(),
        name="mixer",
    )(x2, o, w["norm_mix_pre"], w["w_rest_in"], w["gate_b"], w["sg_ln_g"], w["sg_ln_b"], w["w_sg"],
      w["sg_bias"], w["conv_w"], w["w_pool"], w["pool_scale"], w["w_br"], w["w_out"], w["norm_mix_post"])


def _ffn_kernel(x_ref, g_ref, w1_ref, w2_ref, gpost_ref, out_ref):
    x = x_ref[...]
    h = _rms(x, g_ref[...]).astype(BF16)
    a = jnp.maximum(_dot(h, w1_ref[...]), 0.0)
    f = _dot((a * a).astype(BF16), w2_ref[...])
    out_ref[...] = x + _rms(f, gpost_ref[...])


def _ffn(l, x2, w):
    n = x2.shape[0]
    return pl.pallas_call(
        _ffn_kernel,
        grid=(n // T_FFN,),
        in_specs=[_row_spec(T_FFN, D_MODEL),
                  _layer_spec(l, 1, D_MODEL),
                  _layer_spec(l, D_MODEL, D_FF),
                  _layer_spec(l, D_FF, D_MODEL),
                  _layer_spec(l, 1, D_MODEL)],
        out_specs=_row_spec(T_FFN, D_MODEL),
        out_shape=jax.ShapeDtypeStruct((n, D_MODEL), F32),
        compiler_params=_params(),
        name="ffn",
    )(x2, w["norm_ffn_pre"], w["w_ff1"], w["w_ff2"], w["norm_ffn_post"])


def _gather_cols(w, idx, sign):
    return jnp.take(w, jnp.asarray(idx), axis=-1) * jnp.asarray(sign, F32)


def _attn_in_layout():
    idx = np.zeros(640, np.int32)
    sign = np.zeros(640, np.float32)
    idx[:Q_RANK + KV_RANK] = np.arange(Q_RANK + KV_RANK)
    sign[:Q_RANK + KV_RANK] = 1.0
    r0 = Q_RANK + KV_RANK
    half = ROPE // 2
    for j in range(ROPE):
        idx[384 + NOPE + j] = r0 + j
        sign[384 + NOPE + j] = 1.0
    for j in range(half):
        idx[512 + NOPE + j] = r0 + half + j
        sign[512 + NOPE + j] = -1.0
        idx[512 + NOPE + half + j] = r0 + j
        sign[512 + NOPE + half + j] = 1.0
    return idx, sign


def _q_layout():
    nq = HEADS * HEAD_LANES
    idx = np.zeros(2 * nq, np.int32)
    sign = np.zeros(2 * nq, np.float32)
    half = ROPE // 2
    for h in range(HEADS):
        src = h * (NOPE + ROPE)
        dst = h * HEAD_LANES
        for j in range(NOPE + ROPE):
            idx[dst + j] = src + j
            sign[dst + j] = 1.0
        for j in range(half):
            idx[nq + dst + NOPE + j] = src + NOPE + half + j
            sign[nq + dst + NOPE + j] = -1.0
            idx[nq + dst + NOPE + half + j] = src + NOPE + j
            sign[nq + dst + NOPE + half + j] = 1.0
    return idx, sign


def _k_layout():
    nq = HEADS * HEAD_LANES
    idx = np.zeros(nq, np.int32)
    sign = np.zeros(nq, np.float32)
    for h in range(HEADS):
        src = h * (NOPE + VDIM)
        for j in range(NOPE):
            idx[h * HEAD_LANES + j] = src + j
            sign[h * HEAD_LANES + j] = 1.0
    return idx, sign


def _v_layout():
    idx = np.zeros(HEADS * VDIM, np.int32)
    for h in range(HEADS):
        for j in range(VDIM):
            idx[h * VDIM + j] = h * (NOPE + VDIM) + NOPE + j
    return idx, np.ones(HEADS * VDIM, np.float32)


def _prepare_weights(p):
    L = DEPTH
    row = lambda a: a.reshape(L, 1, a.shape[-1])
    eye = jnp.eye(len(POOL_WINDOWS), dtype=F32)
    group = BR_WIDTH // len(POOL_WINDOWS)
    return {
        "norm_mix_pre": row(p["norm_mix_pre"]),
        "w_attn_in": _gather_cols(p["w_in"], *_attn_in_layout()).astype(BF16),
        "q_norm": row(p["q_norm"]),
        "w_q_t": jnp.swapaxes(_gather_cols(p["w_uq"], *_q_layout()), 1, 2).astype(BF16),
        "kv_norm": row(p["kv_norm"]),
        "w_k": _gather_cols(p["w_ukv"], *_k_layout()).astype(BF16),
        "w_v_t": jnp.swapaxes(_gather_cols(p["w_ukv"], *_v_layout()), 1, 2).astype(BF16),
        "w_rest_in": p["w_in"][:, :, ATTN_COLS:].astype(BF16),
        "gate_b": row(p["gate_b"]),
        "sg_ln_g": row(p["sg_ln_g"]),
        "sg_ln_b": row(p["sg_ln_b"]),
        "w_sg": jnp.transpose(p["sg_w"], (0, 2, 1, 3)).reshape(L, SG_CHUNK, SG_GROUPS * SG_CHUNK).astype(BF16),
        "sg_bias": jnp.repeat(jnp.transpose(p["sg_b"], (0, 2, 1)), SG_WIDTH // SG_GROUPS, axis=-1),
        "conv_w": p["conv_w"],
        "w_pool": jnp.einsum("lgcd,gh->lgchd", p["pool_w"], eye).reshape(L, BR_WIDTH, BR_WIDTH).astype(BF16),
        "pool_scale": row(p["pool_scale"]),
        "w_br": jnp.stack([p["w_br_mla"], p["w_br_sg"], p["w_br_conv"], p["w_br_pool"]], axis=1).astype(BF16),
        "w_out": p["w_out"].astype(BF16),
        "norm_mix_post": row(p["norm_mix_post"]),
        "norm_ffn_pre": row(p["norm_ffn_pre"]),
        "w_ff1": p["w_ff1"].astype(BF16),
        "w_ff2": p["w_ff2"].astype(BF16),
        "norm_ffn_post": row(p["norm_ffn_post"]),
    }


def kernel(x, positions, norm_mix_pre, w_in, gate_b, q_norm, w_uq, kv_norm, w_ukv, w_br_mla, sg_ln_g, sg_ln_b, sg_w, sg_b, w_br_sg, conv_w, w_br_conv, pool_w, pool_scale, w_br_pool, w_out, norm_mix_post, norm_ffn_pre, w_ff1, w_ff2, norm_ffn_post):
    batch, seq, d = x.shape
    w = _prepare_weights(dict(
        norm_mix_pre=norm_mix_pre, w_in=w_in, gate_b=gate_b, q_norm=q_norm, w_uq=w_uq, kv_norm=kv_norm,
        w_ukv=w_ukv, w_br_mla=w_br_mla, sg_ln_g=sg_ln_g, sg_ln_b=sg_ln_b, sg_w=sg_w, sg_b=sg_b,
        w_br_sg=w_br_sg, conv_w=conv_w, w_br_conv=w_br_conv, pool_w=pool_w, pool_scale=pool_scale,
        w_br_pool=w_br_pool, w_out=w_out, norm_mix_post=norm_mix_post, norm_ffn_pre=norm_ffn_pre,
        w_ff1=w_ff1, w_ff2=w_ff2, norm_ffn_post=norm_ffn_post))
    tables = _rope_tables(positions)
    x2 = x.reshape(batch * seq, d)
    for l in range(DEPTH):
        q_t, k, v_t = _attn_proj(l, x2, tables, w)
        o = _attention(q_t, k, v_t, batch, seq)
        x2 = _mixer(l, x2, o, w, seq)
        x2 = _ffn(l, x2, w)
    return x2.reshape(batch, seq, d)
```

```python
import functools

import numpy as np
import jax
import jax.numpy as jnp
from jax import lax
from jax.experimental import pallas as pl
from jax.experimental.pallas import tpu as pltpu

D_MODEL = 1024
DEPTH = 4
HEADS = 4
NOPE = 64
ROPE = 32
VDIM = 64
Q_RANK = 256
KV_RANK = 128
ROPE_BASE = 10000.0
SG_WIDTH = 256
SG_GROUPS = 4
SG_CHUNK = 128
BR_WIDTH = 256
POOL_WINDOWS = (2, 4, 8, 16)
N_BRANCH = 4
D_FF = 4 * D_MODEL
EPS = 1e-6
HALO = 16
HEAD_LANES = 128
ATTN_COLS = Q_RANK + KV_RANK + ROPE
REST_COLS = 2 * SG_WIDTH + 3 * BR_WIDTH + BR_WIDTH + N_BRANCH * D_MODEL

V_ROWS = VDIM + 16
Q_SCALE = (NOPE + ROPE) ** -0.5 * 1.4426950408889634

T_ROPE = 1024
T_ATT = 512
T_MIX = 512
T_FFN = 512

VMEM_LIMIT = 56 * 1024 * 1024

BF16 = jnp.bfloat16
F32 = jnp.float32


def _dot(a, b):
    return jnp.dot(a, b, preferred_element_type=F32)


def _dot_nt(a, b):
    return lax.dot_general(a, b, (((1,), (1,)), ((), ())), preferred_element_type=F32)


def _rms(x, g):
    return x * lax.rsqrt(jnp.mean(x * x, axis=-1, keepdims=True) + EPS) * g


def _params():
    return pltpu.CompilerParams(dimension_semantics=("arbitrary",), vmem_limit_bytes=VMEM_LIMIT)


def _layer_spec(l, *tail):
    zeros = (0,) * len(tail)
    return pl.BlockSpec((None,) + tuple(tail), lambda *_: (l,) + zeros,
                        pipeline_mode=pl.Buffered(1))


def _row_spec(t, n):
    return pl.BlockSpec((t, n), lambda i: (i, 0))


def _rope_table_kernel(pos_ref, freq_ref, cos_ref, sin_ref, cos_t_ref, sin_t_ref):
    ang = pos_ref[...].astype(F32) * freq_ref[...]
    cos = jnp.cos(ang)
    sin = jnp.sin(ang)
    cos_ref[...] = cos
    sin_ref[...] = sin
    cos_t_ref[...] = cos.T
    sin_t_ref[...] = sin.T


def _rope_tables(positions):
    n = positions.size
    inv_freq = ROPE_BASE ** (-jnp.arange(0, ROPE, 2, dtype=F32) / ROPE)
    freq = jnp.zeros((HEAD_LANES,), F32)
    freq = freq.at[NOPE:NOPE + ROPE // 2].set(inv_freq).at[NOPE + ROPE // 2:NOPE + ROPE].set(inv_freq)
    return pl.pallas_call(
        _rope_table_kernel,
        grid=(n // T_ROPE,),
        in_specs=[pl.BlockSpec((T_ROPE, 1), lambda i: (i, 0)),
                  pl.BlockSpec((1, HEAD_LANES), lambda i: (0, 0))],
        out_specs=[_row_spec(T_ROPE, HEAD_LANES), _row_spec(T_ROPE, HEAD_LANES),
                   pl.BlockSpec((HEAD_LANES, T_ROPE), lambda i: (0, i)),
                   pl.BlockSpec((HEAD_LANES, T_ROPE), lambda i: (0, i))],
        out_shape=[jax.ShapeDtypeStruct((n, HEAD_LANES), F32)] * 2
        + [jax.ShapeDtypeStruct((HEAD_LANES, n), F32)] * 2,
        compiler_params=_params(),
        name="rope_tables",
    )(positions.reshape(n, 1), freq.reshape(1, HEAD_LANES))


def _attn_proj_body(x, rows, g_ref, wa_ref, qn_ref, wqt_ref, kvn_ref, wk_ref, wvt_ref,
                    cos_ref, sin_ref, cos_t_ref, sin_t_ref, qt_ref, k_ref, vt_ref):
    t = x.shape[0]
    h = _rms(x, g_ref[...]).astype(BF16)
    pa = _dot(h, wa_ref[...])
    c_q = pa[:, :Q_RANK]
    c_kv = pa[:, Q_RANK:Q_RANK + KV_RANK]
    kr = pa[:, 384:512]
    kr_rot = pa[:, 512:640]
    nq = HEADS * HEAD_LANES

    qlr_t = _dot_nt(wqt_ref[...], _rms(c_q, qn_ref[...]).astype(BF16))
    cos_t4 = jnp.concatenate([cos_t_ref[:, rows]] * HEADS, axis=0)
    sin_t4 = jnp.concatenate([sin_t_ref[:, rows]] * HEADS, axis=0)
    q_t = (qlr_t[:nq] * cos_t4 + qlr_t[nq:] * sin_t4) * Q_SCALE
    qt_ref[:, rows] = q_t.astype(BF16)

    kvn = _rms(c_kv, kvn_ref[...]).astype(BF16)
    k_rope = kr * cos_ref[rows, :] + kr_rot * sin_ref[rows, :]
    k = _dot(kvn, wk_ref[...]) + jnp.concatenate([k_rope] * HEADS, axis=1)
    k_ref[rows, :] = k.astype(BF16)

    v_t = _dot_nt(wvt_ref[...], kvn)
    ones_rows = (lax.broadcasted_iota(jnp.int32, (V_ROWS - VDIM, t), 0) == 0).astype(F32)
    pieces = []
    for hd in range(HEADS):
        pieces += [v_t[hd * VDIM:(hd + 1) * VDIM], ones_rows]
    vt_ref[:, rows] = jnp.concatenate(pieces, axis=0).astype(BF16)


def _attn_proj_kernel(x_ref, *refs):
    _attn_proj_body(x_ref[...], slice(None), *refs)


def _attn_proj_io(l, n, tables, w):
    nq = HEADS * HEAD_LANES
    col_spec = pl.BlockSpec((HEAD_LANES, T_ATT), lambda i: (0, i))
    in_specs = [_layer_spec(l, 1, D_MODEL),
                _layer_spec(l, D_MODEL, 640),
                _layer_spec(l, 1, Q_RANK),
                _layer_spec(l, 2 * nq, Q_RANK),
                _layer_spec(l, 1, KV_RANK),
                _layer_spec(l, KV_RANK, nq),
                _layer_spec(l, HEADS * VDIM, KV_RANK),
                _row_spec(T_ATT, HEAD_LANES),
                _row_spec(T_ATT, HEAD_LANES),
                col_spec, col_spec]
    operands = [w["norm_mix_pre"], w["w_attn_in"], w["q_norm"], w["w_q_t"], w["kv_norm"], w["w_k"],
                w["w_v_t"], *tables]
    out_specs = [pl.BlockSpec((None, nq, T_ATT), lambda i: (i, 0, 0)),
                 _row_spec(T_ATT, nq),
                 pl.BlockSpec((None, HEADS * V_ROWS, T_ATT), lambda i: (i, 0, 0))]
    out_shape = [jax.ShapeDtypeStruct((n // T_ATT, nq, T_ATT), BF16),
                 jax.ShapeDtypeStruct((n, nq), BF16),
                 jax.ShapeDtypeStruct((n // T_ATT, HEADS * V_ROWS, T_ATT), BF16)]
    return in_specs, operands, out_specs, out_shape


def _attn_proj(l, x2, tables, w):
    n = x2.shape[0]
    in_specs, operands, out_specs, out_shape = _attn_proj_io(l, n, tables, w)
    return pl.pallas_call(
        _attn_proj_kernel,
        grid=(n // T_ATT,),
        in_specs=[_row_spec(T_ATT, D_MODEL)] + in_specs,
        out_specs=out_specs,
        out_shape=out_shape,
        compiler_params=_params(),
        name="attn_proj",
    )(x2, *operands)


def _attn_kernel(qt_ref, k_ref, vt_ref, o_ref, m_sc, acc_sc, sa_sc, sb_sc):
    i = pl.program_id(1)
    m_sc[...] = jnp.full(m_sc.shape, -1e30, F32)
    acc_sc[...] = jnp.zeros(acc_sc.shape, F32)

    def scores(j, h):
        off = pl.multiple_of(j * T_ATT, T_ATT)
        k = k_ref[pl.ds(off, T_ATT), h * HEAD_LANES:(h + 1) * HEAD_LANES]
        return _dot(k, qt_ref[h * HEAD_LANES:(h + 1) * HEAD_LANES, :])

    def accumulate(j, h, s_t, masked):
        if masked:
            key = lax.broadcasted_iota(jnp.int32, s_t.shape, 0)
            qry = lax.broadcasted_iota(jnp.int32, s_t.shape, 1)
            s_t = jnp.where(key <= qry, s_t, -jnp.inf)
        m_prev = m_sc[h]
        m_new = jnp.maximum(m_prev, jnp.max(s_t, axis=0, keepdims=True))
        alpha = jnp.exp2(m_prev - m_new)
        p_t = jnp.exp2(s_t - m_new).astype(BF16)
        v_t = vt_ref[j, h * V_ROWS:(h + 1) * V_ROWS, :]
        acc_sc[h] = alpha * acc_sc[h] + _dot(v_t, p_t)
        m_sc[h] = m_new

    def step(j, cur, nxt):
        for h in range(HEADS):
            if nxt is not None:
                nxt[h] = scores(j + 1, h)
            accumulate(j, h, cur[h], nxt is None)

    def pair(jj, carry):
        step(2 * jj, sa_sc, sb_sc)
        step(2 * jj + 1, sb_sc, sa_sc)
        return carry

    for h in range(HEADS):
        sa_sc[h] = scores(0, h)
    lax.fori_loop(0, i // 2, pair, 0)

    @pl.when(i % 2 == 0)
    def _():
        step(i, sa_sc, None)

    @pl.when(i % 2 == 1)
    def _():
        step(i - 1, sa_sc, sb_sc)
        step(i, sb_sc, None)

    outs = []
    for h in range(HEADS):
        acc = acc_sc[h]
        outs.append(acc[:VDIM] / acc[VDIM:VDIM + 1])
    o_ref[...] = jnp.concatenate(outs, axis=0).T.astype(BF16)


def _attention(q_t, k, v_t, batch, seq):
    nq = seq // T_ATT
    cols = HEADS * HEAD_LANES
    return pl.pallas_call(
        _attn_kernel,
        grid=(batch, nq),
        in_specs=[pl.BlockSpec((None, cols, T_ATT), lambda b, i: (b * nq + i, 0, 0)),
                  pl.BlockSpec((seq, cols), lambda b, i: (b, 0)),
                  pl.BlockSpec((nq, HEADS * V_ROWS, T_ATT), lambda b, i: (b, 0, 0))],
        out_specs=pl.BlockSpec((T_ATT, HEADS * VDIM), lambda b, i: (b * nq + i, 0)),
        out_shape=jax.ShapeDtypeStruct((batch * seq, HEADS * VDIM), BF16),
        scratch_shapes=[pltpu.VMEM((HEADS, 1, T_ATT), F32),
                        pltpu.VMEM((HEADS, V_ROWS, T_ATT), F32),
                        pltpu.VMEM((HEADS, T_ATT, T_ATT), F32),
                        pltpu.VMEM((HEADS, T_ATT, T_ATT), F32)],
        compiler_params=pltpu.CompilerParams(dimension_semantics=("arbitrary", "arbitrary"),
                                             vmem_limit_bytes=VMEM_LIMIT),
        name="attention",
    )(q_t, k, v_t)


def _mixer_kernel(tiles_per_seq, x_ref, o_ref, g_ref, wb_ref, gb_ref, lng_ref, lnb_ref, wsg_ref, sgb_ref,
                  cw_ref, wpool_ref, pscale_ref, wbr_a_ref, wbr_b_ref, wbr_c_ref, wbr_d_ref, wout_ref, gpost_ref,
                  out_ref, ez_ref, ep_ref):
    t = x_ref.shape[0]
    i = pl.program_id(0)
    seq_tile = lax.rem(i, tiles_per_seq)

    @pl.when(seq_tile == 0)
    def _():
        ez_ref[0:HALO, :] = jnp.zeros((HALO, BR_WIDTH), F32)
        ep_ref[0:HALO, :] = jnp.zeros((HALO, BR_WIDTH), F32)

    x = x_ref[...]
    h = _rms(x, g_ref[...]).astype(BF16)
    o0 = 2 * SG_WIDTH
    o1 = o0 + 3 * BR_WIDTH
    o2 = o1 + BR_WIDTH

    def gate_pre(br):
        return (_dot(h, wb_ref[:, o2 + br * D_MODEL:o2 + (br + 1) * D_MODEL])
                + gb_ref[:, br * D_MODEL:(br + 1) * D_MODEL])

    uv = _dot(h, wb_ref[:, 0:2 * SG_WIDTH])
    gp_b = gate_pre(1)
    u = jax.nn.gelu(uv[:, :SG_WIDTH])
    gv = jax.nn.gelu(uv[:, SG_WIDTH:])
    mu = jnp.mean(gv, axis=-1, keepdims=True)
    gc = gv - mu
    vln = gc * lax.rsqrt(jnp.mean(gc * gc, axis=-1, keepdims=True) + EPS) * lng_ref[...] + lnb_ref[...]
    vb = vln.astype(BF16)
    trow = lax.broadcasted_iota(jnp.int32, (SG_CHUNK, SG_GROUPS * SG_CHUNK), 0)
    scol = lax.rem(lax.broadcasted_iota(jnp.int32, (SG_CHUNK, SG_GROUPS * SG_CHUNK), 1), SG_CHUNK)
    wsg = jnp.where(scol <= trow, wsg_ref[...], jnp.zeros((), BF16))
    lane_group = lax.broadcasted_iota(jnp.int32, (SG_CHUNK, SG_WIDTH), 1) // (SG_WIDTH // SG_GROUPS)
    mixed_chunks = []
    for c in range(t // SG_CHUNK):
        vc = vb[c * SG_CHUNK:(c + 1) * SG_CHUNK]
        rhs = jnp.concatenate([jnp.where(lane_group == g, vc, jnp.zeros((), BF16))
                               for g in range(SG_GROUPS)], axis=0)
        mixed_chunks.append(_dot(wsg, rhs) + sgb_ref[...])

    cv = _dot(h, wb_ref[:, o0:o0 + 3 * BR_WIDTH])
    gp_c = gate_pre(2)
    y_b = _dot((u * jnp.concatenate(mixed_chunks, axis=0)).astype(BF16), wbr_b_ref[...])
    merged = jax.nn.sigmoid(gp_b) * y_b
    z = cv[:, 2 * BR_WIDTH:] * cv[:, :BR_WIDTH]
    ez_ref[HALO:HALO + t, :] = z
    conv = (cw_ref[0:1, :] * ez_ref[HALO - 2:HALO - 2 + t, :]
            + cw_ref[1:2, :] * ez_ref[HALO - 1:HALO - 1 + t, :]
            + cw_ref[2:3, :] * z)
    ez_ref[0:HALO, :] = ez_ref[t:t + HALO, :]

    ep_ref[HALO:HALO + t, :] = _dot(h, wb_ref[:, o1:o1 + BR_WIDTH])
    gp_d = gate_pre(3)
    y_c = _dot((cv[:, BR_WIDTH:2 * BR_WIDTH] * conv).astype(BF16), wbr_c_ref[...])
    merged = merged + jax.nn.sigmoid(gp_c) * y_c
    tpos = seq_tile * t + lax.broadcasted_iota(jnp.int32, (t, 128), 0) + 1
    lane_lo = lax.broadcasted_iota(jnp.int32, (t, 128), 1) < 64
    pooled = []
    for half, (w_lo, w_hi) in enumerate(((POOL_WINDOWS[0], POOL_WINDOWS[1]),
                                          (POOL_WINDOWS[2], POOL_WINDOWS[3]))):
        cols = slice(half * 128, (half + 1) * 128)
        cur = ep_ref[HALO:HALO + t, cols]
        run = cur
        for d in range(1, w_lo):
            run = run + ep_ref[HALO - d:HALO - d + t, cols]
        sum_lo = run
        for d in range(w_lo, w_hi):
            run = run + ep_ref[HALO - d:HALO - d + t, cols]
        win = jnp.where(lane_lo, w_lo, w_hi)
        cnt = jnp.minimum(tpos, win).astype(F32)
        pooled.append(jnp.where(lane_lo, sum_lo, run) / cnt - cur)
    pooled = jnp.concatenate(pooled, axis=1).astype(BF16)
    ep_ref[0:HALO, :] = ep_ref[t:t + HALO, :]
    mixed_d = _dot(pooled, wpool_ref[...]) * pscale_ref[...]
    gp_a = gate_pre(0)
    y_d = _dot(mixed_d.astype(BF16), wbr_d_ref[...])
    merged = merged + jax.nn.sigmoid(gp_d) * y_d

    y_a = _dot(o_ref[...], wbr_a_ref[...])
    half = t // 2
    for r in range(2):
        rows = slice(r * half, (r + 1) * half)
        m_r = merged[rows] + jax.nn.sigmoid(gp_a[rows]) * y_a[rows]
        mo = _dot(m_r.astype(BF16), wout_ref[...])
        out_ref[rows, :] = x[rows] + _rms(mo, gpost_ref[...])


def _mixer(l, x2, o, w, seq):
    n = x2.shape[0]
    t = T_MIX
    return pl.pallas_call(
        functools.partial(_mixer_kernel, seq // t),
        grid=(n // t,),
        in_specs=[_row_spec(t, D_MODEL),
                  _row_spec(t, HEADS * VDIM),
                  _layer_spec(l, 1, D_MODEL),
                  _layer_spec(l, D_MODEL, REST_COLS),
                  _layer_spec(l, 1, N_BRANCH * D_MODEL),
                  _layer_spec(l, 1, SG_WIDTH),
                  _layer_spec(l, 1, SG_WIDTH),
                  _layer_spec(l, SG_CHUNK, SG_GROUPS * SG_CHUNK),
                  _layer_spec(l, SG_CHUNK, SG_WIDTH),
                  _layer_spec(l, 3, BR_WIDTH),
                  _layer_spec(l, BR_WIDTH, BR_WIDTH),
                  _layer_spec(l, 1, BR_WIDTH),
                  _layer_spec(l, BR_WIDTH, D_MODEL),
                  _layer_spec(l, BR_WIDTH, D_MODEL),
                  _layer_spec(l, BR_WIDTH, D_MODEL),
                  _layer_spec(l, BR_WIDTH, D_MODEL),
                  _layer_spec(l, D_MODEL, D_MODEL),
                  _layer_spec(l, 1, D_MODEL)],
        out_specs=_row_spec(t, D_MODEL),
        out_shape=jax.ShapeDtypeStruct((n, D_MODEL), F32),
        scratch_shapes=[pltpu.VMEM((t + HALO, BR_WIDTH), F32),
                        pltpu.VMEM((t + HALO, BR_WIDTH), F32)],
        compiler_params=_params(),
        name="mixer",
    )(x2, o, w["norm_mix_pre"], w["w_rest_in"], w["gate_b"], w["sg_ln_g"], w["sg_ln_b"], w["w_sg"],
      w["sg_bias"], w["conv_w"], w["w_pool"], w["pool_scale"], w["w_br_mla"], w["w_br_sg"],
      w["w_br_conv"], w["w_br_pool"], w["w_out"], w["norm_mix_post"])


def _ffn_kernel(x_ref, g_ref, w1_ref, w2_ref, gpost_ref, *rest):
    fused = len(rest) > 1
    out_ref = rest[-4] if fused else rest[0]
    x = x_ref[...]
    h = _rms(x, g_ref[...]).astype(BF16)
    a = jnp.maximum(_dot(h, w1_ref[...]), 0.0)
    f = _dot((a * a).astype(BF16), w2_ref[...])
    x_new = x + _rms(f, gpost_ref[...])
    out_ref[...] = x_new
    if fused:
        _attn_proj_body(x_new, slice(None), *rest[:-4], *rest[-3:])


def _ffn(l, x2, w, tables=None):
    n = x2.shape[0]
    in_specs = [_row_spec(T_FFN, D_MODEL),
                _layer_spec(l, 1, D_MODEL),
                _layer_spec(l, D_MODEL, D_FF),
                _layer_spec(l, D_FF, D_MODEL),
                _layer_spec(l, 1, D_MODEL)]
    operands = [x2, w["norm_ffn_pre"], w["w_ff1"], w["w_ff2"], w["norm_ffn_post"]]
    out_specs = [_row_spec(T_FFN, D_MODEL)]
    out_shape = [jax.ShapeDtypeStruct((n, D_MODEL), F32)]
    if tables is not None:
        p_in, p_ops, p_out, p_shape = _attn_proj_io(l + 1, n, tables, w)
        in_specs += p_in
        operands += p_ops
        out_specs += p_out
        out_shape += p_shape
    return pl.pallas_call(
        _ffn_kernel,
        grid=(n // T_FFN,),
        in_specs=in_specs,
        out_specs=out_specs,
        out_shape=out_shape,
        compiler_params=_params(),
        name="ffn" if tables is None else "ffn_proj",
    )(*operands)


def _gather_cols(w, idx, sign):
    return jnp.take(w, jnp.asarray(idx), axis=-1) * jnp.asarray(sign, F32)


def _w_in_layout_kernel(w_ref, wa_ref, wb_ref):
    w = w_ref[...]
    wb_ref[...] = w[:, ATTN_COLS:].astype(BF16)
    r0 = Q_RANK + KV_RANK
    blk = w[:, r0:r0 + HEAD_LANES]
    lane = lax.broadcasted_iota(jnp.int32, blk.shape, 1)
    half = ROPE // 2
    lin = jnp.where((lane >= NOPE) & (lane < NOPE + ROPE), pltpu.roll(blk, NOPE, 1), 0.0)
    rot = jnp.where((lane >= NOPE) & (lane < NOPE + half), -pltpu.roll(blk, NOPE - half, 1),
                    jnp.where((lane >= NOPE + half) & (lane < NOPE + ROPE),
                              pltpu.roll(blk, NOPE + half, 1), 0.0))
    wa_ref[...] = jnp.concatenate([w[:, :r0], lin, rot], axis=1).astype(BF16)


def _w_in_layout(w_in):
    rows = 128
    n_in = w_in.shape[-1]
    return pl.pallas_call(
        _w_in_layout_kernel,
        grid=(DEPTH, D_MODEL // rows),
        in_specs=[pl.BlockSpec((None, rows, n_in), lambda l, r: (l, r, 0))],
        out_specs=[pl.BlockSpec((None, rows, 640), lambda l, r: (l, r, 0)),
                   pl.BlockSpec((None, rows, REST_COLS), lambda l, r: (l, r, 0))],
        out_shape=[jax.ShapeDtypeStruct((DEPTH, D_MODEL, 640), BF16),
                   jax.ShapeDtypeStruct((DEPTH, D_MODEL, REST_COLS), BF16)],
        compiler_params=pltpu.CompilerParams(dimension_semantics=("arbitrary", "arbitrary"),
                                             vmem_limit_bytes=VMEM_LIMIT),
        name="w_in_layout",
    )(w_in)


def _q_layout():
    nq = HEADS * HEAD_LANES
    idx = np.zeros(2 * nq, np.int32)
    sign = np.zeros(2 * nq, np.float32)
    half = ROPE // 2
    for h in range(HEADS):
        src = h * (NOPE + ROPE)
        dst = h * HEAD_LANES
        for j in range(NOPE + ROPE):
            idx[dst + j] = src + j
            sign[dst + j] = 1.0
        for j in range(half):
            idx[nq + dst + NOPE + j] = src + NOPE + half + j
            sign[nq + dst + NOPE + j] = -1.0
            idx[nq + dst + NOPE + half + j] = src + NOPE + j
            sign[nq + dst + NOPE + half + j] = 1.0
    return idx, sign


def _k_layout():
    nq = HEADS * HEAD_LANES
    idx = np.zeros(nq, np.int32)
    sign = np.zeros(nq, np.float32)
    for h in range(HEADS):
        src = h * (NOPE + VDIM)
        for j in range(NOPE):
            idx[h * HEAD_LANES + j] = src + j
            sign[h * HEAD_LANES + j] = 1.0
    return idx, sign


def _v_layout():
    idx = np.zeros(HEADS * VDIM, np.int32)
    for h in range(HEADS):
        for j in range(VDIM):
            idx[h * VDIM + j] = h * (NOPE + VDIM) + NOPE + j
    return idx, np.ones(HEADS * VDIM, np.float32)


def _prepare_weights(p):
    L = DEPTH
    row = lambda a: a.reshape(L, 1, a.shape[-1])
    eye = jnp.eye(len(POOL_WINDOWS), dtype=F32)
    w_attn_in, w_rest_in = _w_in_layout(p["w_in"])
    return {
        "norm_mix_pre": row(p["norm_mix_pre"]),
        "w_attn_in": w_attn_in,
        "q_norm": row(p["q_norm"]),
        "w_q_t": jnp.swapaxes(_gather_cols(p["w_uq"], *_q_layout()), 1, 2).astype(BF16),
        "kv_norm": row(p["kv_norm"]),
        "w_k": _gather_cols(p["w_ukv"], *_k_layout()).astype(BF16),
        "w_v_t": jnp.swapaxes(_gather_cols(p["w_ukv"], *_v_layout()), 1, 2).astype(BF16),
        "w_rest_in": w_rest_in,
        "gate_b": row(p["gate_b"]),
        "sg_ln_g": row(p["sg_ln_g"]),
        "sg_ln_b": row(p["sg_ln_b"]),
        "w_sg": jnp.transpose(p["sg_w"], (0, 2, 1, 3)).reshape(L, SG_CHUNK, SG_GROUPS * SG_CHUNK).astype(BF16),
        "sg_bias": jnp.repeat(jnp.transpose(p["sg_b"], (0, 2, 1)), SG_WIDTH // SG_GROUPS, axis=-1),
        "conv_w": p["conv_w"],
        "w_pool": jnp.einsum("lgcd,gh->lgchd", p["pool_w"], eye).reshape(L, BR_WIDTH, BR_WIDTH).astype(BF16),
        "pool_scale": row(p["pool_scale"]),
        "w_br_mla": p["w_br_mla"].astype(BF16),
        "w_br_sg": p["w_br_sg"].astype(BF16),
        "w_br_conv": p["w_br_conv"].astype(BF16),
        "w_br_pool": p["w_br_pool"].astype(BF16),
        "w_out": p["w_out"].astype(BF16),
        "norm_mix_post": row(p["norm_mix_post"]),
        "norm_ffn_pre": row(p["norm_ffn_pre"]),
        "w_ff1": p["w_ff1"].astype(BF16),
        "w_ff2": p["w_ff2"].astype(BF16),
        "norm_ffn_post": row(p["norm_ffn_post"]),
    }


def kernel(x, positions, norm_mix_pre, w_in, gate_b, q_norm, w_uq, kv_norm, w_ukv, w_br_mla, sg_ln_g, sg_ln_b, sg_w, sg_b, w_br_sg, conv_w, w_br_conv, pool_w, pool_scale, w_br_pool, w_out, norm_mix_post, norm_ffn_pre, w_ff1, w_ff2, norm_ffn_post):
    batch, seq, d = x.shape
    w = _prepare_weights(dict(
        norm_mix_pre=norm_mix_pre, w_in=w_in, gate_b=gate_b, q_norm=q_norm, w_uq=w_uq, kv_norm=kv_norm,
        w_ukv=w_ukv, w_br_mla=w_br_mla, sg_ln_g=sg_ln_g, sg_ln_b=sg_ln_b, sg_w=sg_w, sg_b=sg_b,
        w_br_sg=w_br_sg, conv_w=conv_w, w_br_conv=w_br_conv, pool_w=pool_w, pool_scale=pool_scale,
        w_br_pool=w_br_pool, w_out=w_out, norm_mix_post=norm_mix_post, norm_ffn_pre=norm_ffn_pre,
        w_ff1=w_ff1, w_ff2=w_ff2, norm_ffn_post=norm_ffn_post))
    tables = _rope_tables(positions)
    x2 = x.reshape(batch * seq, d)
    assert T_FFN == T_ATT
    q_t, k, v_t = _attn_proj(0, x2, tables, w)
    for l in range(DEPTH):
        o = _attention(q_t, k, v_t, batch, seq)
        x2 = _mixer(l, x2, o, w, seq)
        if l + 1 < DEPTH:
            x2, q_t, k, v_t = _ffn(l, x2, w, tables)
        else:
            (x2,) = _ffn(l, x2, w)
    return x2.reshape(batch, seq, d)
```

```python
import functools

import numpy as np
import jax
import jax.numpy as jnp
from jax import lax
from jax.experimental import pallas as pl
from jax.experimental.pallas import tpu as pltpu

D_MODEL = 1024
DEPTH = 4
HEADS = 4
NOPE = 64
ROPE = 32
VDIM = 64
Q_RANK = 256
KV_RANK = 128
ROPE_BASE = 10000.0
SG_WIDTH = 256
SG_GROUPS = 4
SG_CHUNK = 128
BR_WIDTH = 256
POOL_WINDOWS = (2, 4, 8, 16)
N_BRANCH = 4
D_FF = 4 * D_MODEL
EPS = 1e-6
HALO = 16
HEAD_LANES = 128
ATTN_COLS = Q_RANK + KV_RANK + ROPE
GATE_COLS0 = 2 * SG_WIDTH + 3 * BR_WIDTH + BR_WIDTH
REST_COLS = GATE_COLS0 + N_BRANCH * D_MODEL
W_COL_BLOCK = 512
assert GATE_COLS0 % W_COL_BLOCK == 0 and REST_COLS % W_COL_BLOCK == 0

V_ROWS = VDIM + 16
Q_SCALE = (NOPE + ROPE) ** -0.5 * 1.4426950408889634

T_ROPE = 1024
T_ATT = 512
T_MIX = 512
T_FFN = 512

VMEM_LIMIT = 56 * 1024 * 1024

BF16 = jnp.bfloat16
F32 = jnp.float32


def _dot(a, b):
    return jnp.dot(a, b, preferred_element_type=F32)


def _dot_nt(a, b):
    return lax.dot_general(a, b, (((1,), (1,)), ((), ())), preferred_element_type=F32)


def _rms(x, g):
    return x * lax.rsqrt(jnp.mean(x * x, axis=-1, keepdims=True) + EPS) * g


def _params():
    return pltpu.CompilerParams(dimension_semantics=("arbitrary",), vmem_limit_bytes=VMEM_LIMIT)


def _layer_spec(l, *tail):
    zeros = (0,) * len(tail)
    return pl.BlockSpec((None,) + tuple(tail), lambda *_: (l,) + zeros,
                        pipeline_mode=pl.Buffered(1))


def _row_spec(t, n):
    return pl.BlockSpec((t, n), lambda i: (i, 0))


def _rope_table_kernel(pos_ref, freq_ref, cos_ref, sin_ref, cos_t_ref, sin_t_ref):
    ang = pos_ref[...].astype(F32) * freq_ref[...]
    cos = jnp.cos(ang)
    sin = jnp.sin(ang)
    cos_ref[...] = cos
    sin_ref[...] = sin
    cos_t_ref[...] = cos.T
    sin_t_ref[...] = sin.T


def _rope_tables(positions):
    n = positions.size
    inv_freq = ROPE_BASE ** (-jnp.arange(0, ROPE, 2, dtype=F32) / ROPE)
    freq = jnp.zeros((HEAD_LANES,), F32)
    freq = freq.at[NOPE:NOPE + ROPE // 2].set(inv_freq).at[NOPE + ROPE // 2:NOPE + ROPE].set(inv_freq)
    return pl.pallas_call(
        _rope_table_kernel,
        grid=(n // T_ROPE,),
        in_specs=[pl.BlockSpec((T_ROPE, 1), lambda i: (i, 0)),
                  pl.BlockSpec((1, HEAD_LANES), lambda i: (0, 0))],
        out_specs=[_row_spec(T_ROPE, HEAD_LANES), _row_spec(T_ROPE, HEAD_LANES),
                   pl.BlockSpec((HEAD_LANES, T_ROPE), lambda i: (0, i)),
                   pl.BlockSpec((HEAD_LANES, T_ROPE), lambda i: (0, i))],
        out_shape=[jax.ShapeDtypeStruct((n, HEAD_LANES), F32)] * 2
        + [jax.ShapeDtypeStruct((HEAD_LANES, n), F32)] * 2,
        compiler_params=_params(),
        name="rope_tables",
    )(positions.reshape(n, 1), freq.reshape(1, HEAD_LANES))


def _attn_proj_body(x, rows, g_ref, wa_ref, qn_ref, wqt_ref, kvn_ref, wk_ref, wvt_ref,
                    cos_ref, sin_ref, cos_t_ref, sin_t_ref, qt_ref, k_ref, vt_ref):
    t = x.shape[0]
    h = _rms(x, g_ref[...]).astype(BF16)
    pa = _dot(h, wa_ref[...])
    c_q = pa[:, :Q_RANK]
    c_kv = pa[:, Q_RANK:Q_RANK + KV_RANK]
    kr = pa[:, 384:512]
    kr_rot = pa[:, 512:640]
    nq = HEADS * HEAD_LANES

    qlr_t = _dot_nt(wqt_ref[...], _rms(c_q, qn_ref[...]).astype(BF16))
    cos_t4 = jnp.concatenate([cos_t_ref[:, rows]] * HEADS, axis=0)
    sin_t4 = jnp.concatenate([sin_t_ref[:, rows]] * HEADS, axis=0)
    q_t = (qlr_t[:nq] * cos_t4 + qlr_t[nq:] * sin_t4) * Q_SCALE
    qt_ref[:, rows] = q_t.astype(BF16)

    kvn = _rms(c_kv, kvn_ref[...]).astype(BF16)
    k_rope = kr * cos_ref[rows, :] + kr_rot * sin_ref[rows, :]
    k = _dot(kvn, wk_ref[...]) + jnp.concatenate([k_rope] * HEADS, axis=1)
    k_ref[rows, :] = k.astype(BF16)

    v_t = _dot_nt(wvt_ref[...], kvn)
    ones_rows = (lax.broadcasted_iota(jnp.int32, (V_ROWS - VDIM, t), 0) == 0).astype(F32)
    pieces = []
    for hd in range(HEADS):
        pieces += [v_t[hd * VDIM:(hd + 1) * VDIM], ones_rows]
    vt_ref[:, rows] = jnp.concatenate(pieces, axis=0).astype(BF16)


def _attn_proj_kernel(x_ref, *refs):
    _attn_proj_body(x_ref[...], slice(None), *refs)


def _attn_proj_io(l, n, tables, w):
    nq = HEADS * HEAD_LANES
    col_spec = pl.BlockSpec((HEAD_LANES, T_ATT), lambda i: (0, i))
    in_specs = [_layer_spec(l, 1, D_MODEL),
                _layer_spec(l, D_MODEL, 640),
                _layer_spec(l, 1, Q_RANK),
                _layer_spec(l, 2 * nq, Q_RANK),
                _layer_spec(l, 1, KV_RANK),
                _layer_spec(l, KV_RANK, nq),
                _layer_spec(l, HEADS * VDIM, KV_RANK),
                _row_spec(T_ATT, HEAD_LANES),
                _row_spec(T_ATT, HEAD_LANES),
                col_spec, col_spec]
    operands = [w["norm_mix_pre"], w["w_attn_in"], w["q_norm"], w["w_q_t"], w["kv_norm"], w["w_k"],
                w["w_v_t"], *tables]
    out_specs = [pl.BlockSpec((None, nq, T_ATT), lambda i: (i, 0, 0)),
                 _row_spec(T_ATT, nq),
                 pl.BlockSpec((None, HEADS * V_ROWS, T_ATT), lambda i: (i, 0, 0))]
    out_shape = [jax.ShapeDtypeStruct((n // T_ATT, nq, T_ATT), BF16),
                 jax.ShapeDtypeStruct((n, nq), BF16),
                 jax.ShapeDtypeStruct((n // T_ATT, HEADS * V_ROWS, T_ATT), BF16)]
    return in_specs, operands, out_specs, out_shape


def _attn_proj(l, x2, tables, w):
    n = x2.shape[0]
    in_specs, operands, out_specs, out_shape = _attn_proj_io(l, n, tables, w)
    return pl.pallas_call(
        _attn_proj_kernel,
        grid=(n // T_ATT,),
        in_specs=[_row_spec(T_ATT, D_MODEL)] + in_specs,
        out_specs=out_specs,
        out_shape=out_shape,
        compiler_params=_params(),
        name="attn_proj",
    )(x2, *operands)


def _attn_kernel(qt_ref, k_ref, vt_ref, o_ref, m_sc, acc_sc, sa_sc, sb_sc):
    i = pl.program_id(1)
    m_sc[...] = jnp.full(m_sc.shape, -1e30, F32)
    acc_sc[...] = jnp.zeros(acc_sc.shape, F32)

    def scores(j, h):
        off = pl.multiple_of(j * T_ATT, T_ATT)
        k = k_ref[pl.ds(off, T_ATT), h * HEAD_LANES:(h + 1) * HEAD_LANES]
        return _dot(k, qt_ref[h * HEAD_LANES:(h + 1) * HEAD_LANES, :])

    def accumulate(j, h, s_t, masked):
        if masked:
            key = lax.broadcasted_iota(jnp.int32, s_t.shape, 0)
            qry = lax.broadcasted_iota(jnp.int32, s_t.shape, 1)
            s_t = jnp.where(key <= qry, s_t, -jnp.inf)
        m_prev = m_sc[h]
        m_new = jnp.maximum(m_prev, jnp.max(s_t, axis=0, keepdims=True))
        alpha = jnp.exp2(m_prev - m_new)
        p_t = jnp.exp2(s_t - m_new).astype(BF16)
        v_t = vt_ref[j, h * V_ROWS:(h + 1) * V_ROWS, :]
        acc_sc[h] = alpha * acc_sc[h] + _dot(v_t, p_t)
        m_sc[h] = m_new

    def step(j, cur, nxt):
        for h in range(HEADS):
            if nxt is not None:
                nxt[h] = scores(j + 1, h)
            accumulate(j, h, cur[h], nxt is None)

    def pair(jj, carry):
        step(2 * jj, sa_sc, sb_sc)
        step(2 * jj + 1, sb_sc, sa_sc)
        return carry

    for h in range(HEADS):
        sa_sc[h] = scores(0, h)
    lax.fori_loop(0, i // 2, pair, 0)

    @pl.when(i % 2 == 0)
    def _():
        step(i, sa_sc, None)

    @pl.when(i % 2 == 1)
    def _():
        step(i - 1, sa_sc, sb_sc)
        step(i, sb_sc, None)

    outs = []
    for h in range(HEADS):
        acc = acc_sc[h]
        outs.append(acc[:VDIM] / acc[VDIM:VDIM + 1])
    o_ref[...] = jnp.concatenate(outs, axis=0).T.astype(BF16)


def _attention(q_t, k, v_t, batch, seq):
    nq = seq // T_ATT
    cols = HEADS * HEAD_LANES
    return pl.pallas_call(
        _attn_kernel,
        grid=(batch, nq),
        in_specs=[pl.BlockSpec((None, cols, T_ATT), lambda b, i: (b * nq + i, 0, 0)),
                  pl.BlockSpec((seq, cols), lambda b, i: (b, 0)),
                  pl.BlockSpec((nq, HEADS * V_ROWS, T_ATT), lambda b, i: (b, 0, 0))],
        out_specs=pl.BlockSpec((T_ATT, HEADS * VDIM), lambda b, i: (b * nq + i, 0)),
        out_shape=jax.ShapeDtypeStruct((batch * seq, HEADS * VDIM), BF16),
        scratch_shapes=[pltpu.VMEM((HEADS, 1, T_ATT), F32),
                        pltpu.VMEM((HEADS, V_ROWS, T_ATT), F32),
                        pltpu.VMEM((HEADS, T_ATT, T_ATT), F32),
                        pltpu.VMEM((HEADS, T_ATT, T_ATT), F32)],
        compiler_params=pltpu.CompilerParams(dimension_semantics=("arbitrary", "arbitrary"),
                                             vmem_limit_bytes=VMEM_LIMIT),
        name="attention",
    )(q_t, k, v_t)


def _mixer_kernel(tiles_per_seq, x_ref, o_ref, g_ref, wb_ref, gb_ref, lng_ref, lnb_ref, wsg_ref, sgb_ref,
                  cw_ref, wpool_ref, pscale_ref, wbr_a_ref, wbr_b_ref, wbr_c_ref, wbr_d_ref, wout_ref, gpost_ref,
                  out_ref, ez_ref, ep_ref):
    t = x_ref.shape[0]
    i = pl.program_id(0)
    seq_tile = lax.rem(i, tiles_per_seq)

    @pl.when(seq_tile == 0)
    def _():
        ez_ref[0:HALO, :] = jnp.zeros((HALO, BR_WIDTH), F32)
        ep_ref[0:HALO, :] = jnp.zeros((HALO, BR_WIDTH), F32)

    x = x_ref[...]
    h = _rms(x, g_ref[...]).astype(BF16)
    o0 = 2 * SG_WIDTH
    o1 = o0 + 3 * BR_WIDTH
    o2 = o1 + BR_WIDTH

    def gate_pre(br):
        return (_dot(h, wb_ref[:, o2 + br * D_MODEL:o2 + (br + 1) * D_MODEL])
                + gb_ref[:, br * D_MODEL:(br + 1) * D_MODEL])

    def gated(gp, y):
        return (1.0 + jnp.tanh(gp)) * y

    uv = _dot(h, wb_ref[:, 0:2 * SG_WIDTH])
    gp_b = gate_pre(1)
    u = jax.nn.gelu(uv[:, :SG_WIDTH])
    gv = jax.nn.gelu(uv[:, SG_WIDTH:])
    mu = jnp.mean(gv, axis=-1, keepdims=True)
    gc = gv - mu
    vln = gc * lax.rsqrt(jnp.mean(gc * gc, axis=-1, keepdims=True) + EPS) * lng_ref[...] + lnb_ref[...]
    vb = vln.astype(BF16)
    trow = lax.broadcasted_iota(jnp.int32, (SG_CHUNK, SG_GROUPS * SG_CHUNK), 0)
    scol = lax.rem(lax.broadcasted_iota(jnp.int32, (SG_CHUNK, SG_GROUPS * SG_CHUNK), 1), SG_CHUNK)
    wsg = jnp.where(scol <= trow, wsg_ref[...], jnp.zeros((), BF16))
    lane_group = lax.broadcasted_iota(jnp.int32, (SG_CHUNK, SG_WIDTH), 1) // (SG_WIDTH // SG_GROUPS)
    mixed_chunks = []
    for c in range(t // SG_CHUNK):
        vc = vb[c * SG_CHUNK:(c + 1) * SG_CHUNK]
        rhs = jnp.concatenate([jnp.where(lane_group == g, vc, jnp.zeros((), BF16))
                               for g in range(SG_GROUPS)], axis=0)
        mixed_chunks.append(_dot(wsg, rhs) + sgb_ref[...])

    cv = _dot(h, wb_ref[:, o0:o0 + 3 * BR_WIDTH])
    gp_c = gate_pre(2)
    y_b = _dot((u * jnp.concatenate(mixed_chunks, axis=0)).astype(BF16), wbr_b_ref[...])
    merged = gated(gp_b, y_b)
    z = cv[:, 2 * BR_WIDTH:] * cv[:, :BR_WIDTH]
    ez_ref[HALO:HALO + t, :] = z
    conv = (cw_ref[0:1, :] * ez_ref[HALO - 2:HALO - 2 + t, :]
            + cw_ref[1:2, :] * ez_ref[HALO - 1:HALO - 1 + t, :]
            + cw_ref[2:3, :] * z)
    ez_ref[0:HALO, :] = ez_ref[t:t + HALO, :]

    ep_ref[HALO:HALO + t, :] = _dot(h, wb_ref[:, o1:o1 + BR_WIDTH])
    gp_d = gate_pre(3)
    y_c = _dot((cv[:, BR_WIDTH:2 * BR_WIDTH] * conv).astype(BF16), wbr_c_ref[...])
    merged = merged + gated(gp_c, y_c)
    tpos = seq_tile * t + lax.broadcasted_iota(jnp.int32, (t, 128), 0) + 1
    lane_lo = lax.broadcasted_iota(jnp.int32, (t, 128), 1) < 64
    pooled = []
    for half, (w_lo, w_hi) in enumerate(((POOL_WINDOWS[0], POOL_WINDOWS[1]),
                                          (POOL_WINDOWS[2], POOL_WINDOWS[3]))):
        cols = slice(half * 128, (half + 1) * 128)
        cur = ep_ref[HALO:HALO + t, cols]
        run = cur
        for d in range(1, w_lo):
            run = run + ep_ref[HALO - d:HALO - d + t, cols]
        sum_lo = run
        for d in range(w_lo, w_hi):
            run = run + ep_ref[HALO - d:HALO - d + t, cols]
        win = jnp.where(lane_lo, w_lo, w_hi)
        cnt = jnp.minimum(tpos, win).astype(F32)
        pooled.append(jnp.where(lane_lo, sum_lo, run) / cnt - cur)
    pooled = jnp.concatenate(pooled, axis=1).astype(BF16)
    ep_ref[0:HALO, :] = ep_ref[t:t + HALO, :]
    mixed_d = _dot(pooled, wpool_ref[...]) * pscale_ref[...]
    gp_a = gate_pre(0)
    y_d = _dot(mixed_d.astype(BF16), wbr_d_ref[...])
    merged = merged + gated(gp_d, y_d)

    y_a = _dot(o_ref[...], wbr_a_ref[...])
    half = t // 2
    for r in range(2):
        rows = slice(r * half, (r + 1) * half)
        m_r = merged[rows] + gated(gp_a[rows], y_a[rows])
        mo = _dot(m_r.astype(BF16), wout_ref[...])
        out_ref[rows, :] = x[rows] + _rms(mo, gpost_ref[...])


def _mixer(l, x2, o, w, seq):
    n = x2.shape[0]
    t = T_MIX
    return pl.pallas_call(
        functools.partial(_mixer_kernel, seq // t),
        grid=(n // t,),
        in_specs=[_row_spec(t, D_MODEL),
                  _row_spec(t, HEADS * VDIM),
                  _layer_spec(l, 1, D_MODEL),
                  _layer_spec(l, D_MODEL, REST_COLS),
                  _layer_spec(l, 1, N_BRANCH * D_MODEL),
                  _layer_spec(l, 1, SG_WIDTH),
                  _layer_spec(l, 1, SG_WIDTH),
                  _layer_spec(l, SG_CHUNK, SG_GROUPS * SG_CHUNK),
                  _layer_spec(l, SG_CHUNK, SG_WIDTH),
                  _layer_spec(l, 3, BR_WIDTH),
                  _layer_spec(l, BR_WIDTH, BR_WIDTH),
                  _layer_spec(l, 1, BR_WIDTH),
                  _layer_spec(l, BR_WIDTH, D_MODEL),
                  _layer_spec(l, BR_WIDTH, D_MODEL),
                  _layer_spec(l, BR_WIDTH, D_MODEL),
                  _layer_spec(l, BR_WIDTH, D_MODEL),
                  _layer_spec(l, D_MODEL, D_MODEL),
                  _layer_spec(l, 1, D_MODEL)],
        out_specs=_row_spec(t, D_MODEL),
        out_shape=jax.ShapeDtypeStruct((n, D_MODEL), F32),
        scratch_shapes=[pltpu.VMEM((t + HALO, BR_WIDTH), F32),
                        pltpu.VMEM((t + HALO, BR_WIDTH), F32)],
        compiler_params=_params(),
        name="mixer",
    )(x2, o, w["norm_mix_pre"], w["w_rest_in"], w["gate_b"], w["sg_ln_g"], w["sg_ln_b"], w["w_sg"],
      w["sg_bias"], w["conv_w"], w["w_pool"], w["pool_scale"], w["w_br_mla"], w["w_br_sg"],
      w["w_br_conv"], w["w_br_pool"], w["w_out"], w["norm_mix_post"])


def _ffn_kernel(x_ref, g_ref, w1_ref, w2_ref, gpost_ref, *rest):
    fused = len(rest) > 1
    out_ref = rest[-4] if fused else rest[0]
    x = x_ref[...]
    h = _rms(x, g_ref[...]).astype(BF16)
    a = jnp.maximum(_dot(h, w1_ref[...]), 0.0)
    f = _dot((a * a).astype(BF16), w2_ref[...])
    x_new = x + _rms(f, gpost_ref[...])
    out_ref[...] = x_new
    if fused:
        _attn_proj_body(x_new, slice(None), *rest[:-4], *rest[-3:])


def _ffn(l, x2, w, tables=None):
    n = x2.shape[0]
    in_specs = [_row_spec(T_FFN, D_MODEL),
                _layer_spec(l, 1, D_MODEL),
                _layer_spec(l, D_MODEL, D_FF),
                _layer_spec(l, D_FF, D_MODEL),
                _layer_spec(l, 1, D_MODEL)]
    operands = [x2, w["norm_ffn_pre"], w["w_ff1"], w["w_ff2"], w["norm_ffn_post"]]
    out_specs = [_row_spec(T_FFN, D_MODEL)]
    out_shape = [jax.ShapeDtypeStruct((n, D_MODEL), F32)]
    if tables is not None:
        p_in, p_ops, p_out, p_shape = _attn_proj_io(l + 1, n, tables, w)
        in_specs += p_in
        operands += p_ops
        out_specs += p_out
        out_shape += p_shape
    return pl.pallas_call(
        _ffn_kernel,
        grid=(n // T_FFN,),
        in_specs=in_specs,
        out_specs=out_specs,
        out_shape=out_shape,
        compiler_params=_params(),
        name="ffn" if tables is None else "ffn_proj",
    )(*operands)


def _gather_cols(w, idx, sign):
    return jnp.take(w, jnp.asarray(idx), axis=-1) * jnp.asarray(sign, F32)


def _w_attn_layout_kernel(wt_ref, wa_ref):
    w = wt_ref[...].T
    r0 = Q_RANK + KV_RANK
    blk = w[:, r0:r0 + HEAD_LANES]
    lane = lax.broadcasted_iota(jnp.int32, blk.shape, 1)
    half = ROPE // 2
    lin = jnp.where((lane >= NOPE) & (lane < NOPE + ROPE), pltpu.roll(blk, NOPE, 1), 0.0)
    rot = jnp.where((lane >= NOPE) & (lane < NOPE + half), -pltpu.roll(blk, NOPE - half, 1),
                    jnp.where((lane >= NOPE + half) & (lane < NOPE + ROPE),
                              pltpu.roll(blk, NOPE + half, 1), 0.0))
    wa_ref[...] = jnp.concatenate([w[:, :r0], lin, rot], axis=1).astype(BF16)


def _w_rest_layout_kernel(wt_ref, wb_ref):
    scale = jnp.where(pl.program_id(1) >= GATE_COLS0 // W_COL_BLOCK, 0.5, 1.0)
    wb_ref[...] = (wt_ref[0] * scale).T.astype(BF16)


def _w_in_layout(w_in):
    w_t = jnp.swapaxes(w_in, 1, 2)
    params = pltpu.CompilerParams(dimension_semantics=("arbitrary", "arbitrary"), vmem_limit_bytes=VMEM_LIMIT)
    w_attn = pl.pallas_call(
        _w_attn_layout_kernel,
        grid=(DEPTH, 1),
        in_specs=[pl.BlockSpec((None, 4 * HEAD_LANES, D_MODEL), lambda l, c: (l, 0, 0))],
        out_specs=pl.BlockSpec((None, D_MODEL, 640), lambda l, c: (l, 0, 0)),
        out_shape=jax.ShapeDtypeStruct((DEPTH, D_MODEL, 640), BF16),
        compiler_params=params,
        name="w_attn_layout",
    )(w_t)
    w_rest = pl.pallas_call(
        _w_rest_layout_kernel,
        grid=(DEPTH, REST_COLS // W_COL_BLOCK),
        in_specs=[pl.BlockSpec((pl.Element(1), pl.Element(W_COL_BLOCK), pl.Element(D_MODEL)),
                               lambda l, c: (l, pl.multiple_of(ATTN_COLS + c * W_COL_BLOCK, 32), 0))],
        out_specs=pl.BlockSpec((None, D_MODEL, W_COL_BLOCK), lambda l, c: (l, 0, c)),
        out_shape=jax.ShapeDtypeStruct((DEPTH, D_MODEL, REST_COLS), BF16),
        compiler_params=params,
        name="w_rest_layout",
    )(w_t)
    return w_attn, w_rest


def _q_layout():
    nq = HEADS * HEAD_LANES
    idx = np.zeros(2 * nq, np.int32)
    sign = np.zeros(2 * nq, np.float32)
    half = ROPE // 2
    for h in range(HEADS):
        src = h * (NOPE + ROPE)
        dst = h * HEAD_LANES
        for j in range(NOPE + ROPE):
            idx[dst + j] = src + j
            sign[dst + j] = 1.0
        for j in range(half):
            idx[nq + dst + NOPE + j] = src + NOPE + half + j
            sign[nq + dst + NOPE + j] = -1.0
            idx[nq + dst + NOPE + half + j] = src + NOPE + j
            sign[nq + dst + NOPE + half + j] = 1.0
    return idx, sign


def _k_layout():
    nq = HEADS * HEAD_LANES
    idx = np.zeros(nq, np.int32)
    sign = np.zeros(nq, np.float32)
    for h in range(HEADS):
        src = h * (NOPE + VDIM)
        for j in range(NOPE):
            idx[h * HEAD_LANES + j] = src + j
            sign[h * HEAD_LANES + j] = 1.0
    return idx, sign


def _v_layout():
    idx = np.zeros(HEADS * VDIM, np.int32)
    for h in range(HEADS):
        for j in range(VDIM):
            idx[h * VDIM + j] = h * (NOPE + VDIM) + NOPE + j
    return idx, np.ones(HEADS * VDIM, np.float32)


def _prepare_weights(p):
    L = DEPTH
    row = lambda a: a.reshape(L, 1, a.shape[-1])
    eye = jnp.eye(len(POOL_WINDOWS), dtype=F32)
    w_attn_in, w_rest_in = _w_in_layout(p["w_in"])
    return {
        "norm_mix_pre": row(p["norm_mix_pre"]),
        "w_attn_in": w_attn_in,
        "q_norm": row(p["q_norm"]),
        "w_q_t": jnp.swapaxes(_gather_cols(p["w_uq"], *_q_layout()), 1, 2).astype(BF16),
        "kv_norm": row(p["kv_norm"]),
        "w_k": _gather_cols(p["w_ukv"], *_k_layout()).astype(BF16),
        "w_v_t": jnp.swapaxes(_gather_cols(p["w_ukv"], *_v_layout()), 1, 2).astype(BF16),
        "w_rest_in": w_rest_in,
        "gate_b": row(p["gate_b"]) * 0.5,
        "sg_ln_g": row(p["sg_ln_g"]),
        "sg_ln_b": row(p["sg_ln_b"]),
        "w_sg": jnp.transpose(p["sg_w"], (0, 2, 1, 3)).reshape(L, SG_CHUNK, SG_GROUPS * SG_CHUNK).astype(BF16),
        "sg_bias": jnp.repeat(jnp.transpose(p["sg_b"], (0, 2, 1)), SG_WIDTH // SG_GROUPS, axis=-1),
        "conv_w": p["conv_w"],
        "w_pool": jnp.einsum("lgcd,gh->lgchd", p["pool_w"], eye).reshape(L, BR_WIDTH, BR_WIDTH).astype(BF16),
        "pool_scale": row(p["pool_scale"]),
        "w_br_mla": (p["w_br_mla"] * 0.5).astype(BF16),
        "w_br_sg": (p["w_br_sg"] * 0.5).astype(BF16),
        "w_br_conv": (p["w_br_conv"] * 0.5).astype(BF16),
        "w_br_pool": (p["w_br_pool"] * 0.5).astype(BF16),
        "w_out": p["w_out"].astype(BF16),
        "norm_mix_post": row(p["norm_mix_post"]),
        "norm_ffn_pre": row(p["norm_ffn_pre"]),
        "w_ff1": p["w_ff1"].astype(BF16),
        "w_ff2": p["w_ff2"].astype(BF16),
        "norm_ffn_post": row(p["norm_ffn_post"]),
    }


def kernel(x, positions, norm_mix_pre, w_in, gate_b, q_norm, w_uq, kv_norm, w_ukv, w_br_mla, sg_ln_g, sg_ln_b, sg_w, sg_b, w_br_sg, conv_w, w_br_conv, pool_w, pool_scale, w_br_pool, w_out, norm_mix_post, norm_ffn_pre, w_ff1, w_ff2, norm_ffn_post):
    batch, seq, d = x.shape
    w = _prepare_weights(dict(
        norm_mix_pre=norm_mix_pre, w_in=w_in, gate_b=gate_b, q_norm=q_norm, w_uq=w_uq, kv_norm=kv_norm,
        w_ukv=w_ukv, w_br_mla=w_br_mla, sg_ln_g=sg_ln_g, sg_ln_b=sg_ln_b, sg_w=sg_w, sg_b=sg_b,
        w_br_sg=w_br_sg, conv_w=conv_w, w_br_conv=w_br_conv, pool_w=pool_w, pool_scale=pool_scale,
        w_br_pool=w_br_pool, w_out=w_out, norm_mix_post=norm_mix_post, norm_ffn_pre=norm_ffn_pre,
        w_ff1=w_ff1, w_ff2=w_ff2, norm_ffn_post=norm_ffn_post))
    tables = _rope_tables(positions)
    x2 = x.reshape(batch * seq, d)
    assert T_FFN == T_ATT
    q_t, k, v_t = _attn_proj(0, x2, tables, w)
    for l in range(DEPTH):
        o = _attention(q_t, k, v_t, batch, seq)
        x2 = _mixer(l, x2, o, w, seq)
        if l + 1 < DEPTH:
            x2, q_t, k, v_t = _ffn(l, x2, w, tables)
        else:
            (x2,) = _ffn(l, x2, w)
    return x2.reshape(batch, seq, d)
```

```python
import functools

import numpy as np
import jax
import jax.numpy as jnp
from jax import lax
from jax.experimental import pallas as pl
from jax.experimental.pallas import tpu as pltpu

D_MODEL = 1024
DEPTH = 4
HEADS = 4
NOPE = 64
ROPE = 32
VDIM = 64
Q_RANK = 256
KV_RANK = 128
ROPE_BASE = 10000.0
SG_WIDTH = 256
SG_GROUPS = 4
SG_CHUNK = 128
BR_WIDTH = 256
POOL_WINDOWS = (2, 4, 8, 16)
N_BRANCH = 4
D_FF = 4 * D_MODEL
EPS = 1e-6
HALO = 16
HEAD_LANES = 128
ATTN_COLS = Q_RANK + KV_RANK + ROPE
ATTN_IN_COLS = Q_RANK + KV_RANK + HEAD_LANES
GATE_COLS0 = 2 * SG_WIDTH + 3 * BR_WIDTH + BR_WIDTH
REST_COLS = GATE_COLS0 + N_BRANCH * D_MODEL
W_COL_BLOCK = 512
assert GATE_COLS0 % W_COL_BLOCK == 0 and REST_COLS % W_COL_BLOCK == 0

V_ROWS = VDIM + 16
Q_SCALE = (NOPE + ROPE) ** -0.5 * 1.4426950408889634

T_ROPE = 1024
T_ATT = 512
T_MIX = 512
T_FFN = 512

VMEM_LIMIT = 56 * 1024 * 1024

BF16 = jnp.bfloat16
F32 = jnp.float32


def _dot(a, b):
    return jnp.dot(a, b, preferred_element_type=F32)


def _dot_nt(a, b):
    return lax.dot_general(a, b, (((1,), (1,)), ((), ())), preferred_element_type=F32)


def _rms(x, g):
    return x * lax.rsqrt(jnp.mean(x * x, axis=-1, keepdims=True) + EPS) * g


def _params():
    return pltpu.CompilerParams(dimension_semantics=("arbitrary",), vmem_limit_bytes=VMEM_LIMIT)


def _layer_spec(l, *tail):
    zeros = (0,) * len(tail)
    return pl.BlockSpec((None,) + tuple(tail), lambda *_: (l,) + zeros,
                        pipeline_mode=pl.Buffered(1))


def _row_spec(t, n):
    return pl.BlockSpec((t, n), lambda i: (i, 0))


def _rope_table_kernel(pos_ref, freq_ref, cos_ref, sin_ref, cos_t_ref, sin_t_ref):
    t = pos_ref.shape[1]
    ang = freq_ref[...] * pos_ref[...].astype(F32)
    ones = lambda n: jnp.ones((n, t), F32)
    zeros = lambda n: jnp.zeros((n, t), F32)
    pad = HEAD_LANES - NOPE - ROPE
    cos_t = jnp.concatenate([ones(NOPE), jnp.cos(ang), ones(pad)], axis=0)
    sin_t = jnp.concatenate([zeros(NOPE), jnp.sin(ang), zeros(pad)], axis=0)
    cos_t_ref[...] = cos_t
    sin_t_ref[...] = sin_t
    cos_ref[...] = cos_t.T
    sin_ref[...] = sin_t.T


def _rope_tables(positions):
    n = positions.size
    inv_freq = ROPE_BASE ** (-jnp.arange(0, ROPE, 2, dtype=F32) / ROPE)
    freq = jnp.concatenate([inv_freq, inv_freq]).reshape(ROPE, 1)
    return pl.pallas_call(
        _rope_table_kernel,
        grid=(n // T_ROPE,),
        in_specs=[pl.BlockSpec((1, T_ROPE), lambda i: (0, i)),
                  pl.BlockSpec((ROPE, 1), lambda i: (0, 0))],
        out_specs=[_row_spec(T_ROPE, HEAD_LANES), _row_spec(T_ROPE, HEAD_LANES),
                   pl.BlockSpec((HEAD_LANES, T_ROPE), lambda i: (0, i)),
                   pl.BlockSpec((HEAD_LANES, T_ROPE), lambda i: (0, i))],
        out_shape=[jax.ShapeDtypeStruct((n, HEAD_LANES), F32)] * 2
        + [jax.ShapeDtypeStruct((HEAD_LANES, n), F32)] * 2,
        compiler_params=_params(),
        name="rope_tables",
    )(positions.reshape(1, n), freq)


def _attn_proj_body(x, rows, g_ref, wa_ref, qn_ref, wqt_ref, kvn_ref, wk_ref, wvt_ref,
                    cos_ref, sin_ref, cos_t_ref, sin_t_ref, qt_ref, k_ref, vt_ref):
    t = x.shape[0]
    h = _rms(x, g_ref[...]).astype(BF16)
    pa = _dot(h, wa_ref[...])
    c_q = pa[:, :Q_RANK]
    c_kv = pa[:, Q_RANK:Q_RANK + KV_RANK]
    kr = pa[:, Q_RANK + KV_RANK:]
    nq = HEADS * HEAD_LANES

    qlr_t = _dot_nt(wqt_ref[...], _rms(c_q, qn_ref[...]).astype(BF16))
    cos_t = cos_t_ref[:, rows]
    sin_rope_t = sin_t_ref[NOPE:NOPE + ROPE, rows]
    zeros = lambda n: jnp.zeros((n, t), F32)
    q_heads = []
    for hd in range(HEADS):
        rot = qlr_t[nq + hd * ROPE:nq + (hd + 1) * ROPE] * sin_rope_t
        q_heads.append(qlr_t[hd * HEAD_LANES:(hd + 1) * HEAD_LANES] * cos_t
                       + jnp.concatenate([zeros(NOPE), rot, zeros(HEAD_LANES - NOPE - ROPE)], axis=0))
    qt_ref[:, rows] = (jnp.concatenate(q_heads, axis=0) * Q_SCALE).astype(BF16)

    kvn = _rms(c_kv, kvn_ref[...]).astype(BF16)
    lane = lax.broadcasted_iota(jnp.int32, kr.shape, 1)
    k_rope = jnp.where(lane < NOPE + ROPE,
                       kr * cos_ref[rows, :] + pltpu.roll(kr, HEAD_LANES - ROPE, 1) * sin_ref[rows, :],
                       0.0)
    k = _dot(kvn, wk_ref[...]) + jnp.concatenate([k_rope] * HEADS, axis=1)
    k_ref[rows, :] = k.astype(BF16)

    v_t = _dot_nt(wvt_ref[...], kvn)
    ones_rows = (lax.broadcasted_iota(jnp.int32, (V_ROWS - VDIM, t), 0) == 0).astype(F32)
    pieces = []
    for hd in range(HEADS):
        pieces += [v_t[hd * VDIM:(hd + 1) * VDIM], ones_rows]
    vt_ref[:, rows] = jnp.concatenate(pieces, axis=0).astype(BF16)


def _attn_proj_kernel(x_ref, *refs):
    _attn_proj_body(x_ref[...], slice(None), *refs)


def _attn_proj_io(l, n, tables, w):
    nq = HEADS * HEAD_LANES
    col_spec = pl.BlockSpec((HEAD_LANES, T_ATT), lambda i: (0, i))
    in_specs = [_layer_spec(l, 1, D_MODEL),
                _layer_spec(l, D_MODEL, ATTN_IN_COLS),
                _layer_spec(l, 1, Q_RANK),
                _layer_spec(l, nq + HEADS * ROPE, Q_RANK),
                _layer_spec(l, 1, KV_RANK),
                _layer_spec(l, KV_RANK, nq),
                _layer_spec(l, HEADS * VDIM, KV_RANK),
                _row_spec(T_ATT, HEAD_LANES),
                _row_spec(T_ATT, HEAD_LANES),
                col_spec, col_spec]
    operands = [w["norm_mix_pre"], w["w_attn_in"], w["q_norm"], w["w_q_t"], w["kv_norm"], w["w_k"],
                w["w_v_t"], *tables]
    out_specs = [pl.BlockSpec((None, nq, T_ATT), lambda i: (i, 0, 0)),
                 _row_spec(T_ATT, nq),
                 pl.BlockSpec((None, HEADS * V_ROWS, T_ATT), lambda i: (i, 0, 0))]
    out_shape = [jax.ShapeDtypeStruct((n // T_ATT, nq, T_ATT), BF16),
                 jax.ShapeDtypeStruct((n, nq), BF16),
                 jax.ShapeDtypeStruct((n // T_ATT, HEADS * V_ROWS, T_ATT), BF16)]
    return in_specs, operands, out_specs, out_shape


def _attn_proj(l, x2, tables, w):
    n = x2.shape[0]
    in_specs, operands, out_specs, out_shape = _attn_proj_io(l, n, tables, w)
    return pl.pallas_call(
        _attn_proj_kernel,
        grid=(n // T_ATT,),
        in_specs=[_row_spec(T_ATT, D_MODEL)] + in_specs,
        out_specs=out_specs,
        out_shape=out_shape,
        compiler_params=_params(),
        name="attn_proj",
    )(x2, *operands)


def _attn_kernel(qt_ref, k_ref, vt_ref, o_ref, m_sc, acc_sc, sa_sc, sb_sc, ma_sc, mb_sc):
    i = pl.program_id(1)
    m_sc[...] = jnp.full(m_sc.shape, -1e30, F32)
    acc_sc[...] = jnp.zeros(acc_sc.shape, F32)

    def produce(j, h, s_ref, bm_ref):
        off = pl.multiple_of(j * T_ATT, T_ATT)
        k = k_ref[pl.ds(off, T_ATT), h * HEAD_LANES:(h + 1) * HEAD_LANES]
        s_t = _dot(k, qt_ref[h * HEAD_LANES:(h + 1) * HEAD_LANES, :])
        s_ref[h] = s_t
        bm_ref[h] = jnp.max(s_t, axis=0, keepdims=True)

    def accumulate(j, h, s_ref, bm_ref, masked):
        s_t = s_ref[h]
        if masked:
            key = lax.broadcasted_iota(jnp.int32, s_t.shape, 0)
            qry = lax.broadcasted_iota(jnp.int32, s_t.shape, 1)
            s_t = jnp.where(key <= qry, s_t, -jnp.inf)
            blk_max = jnp.max(s_t, axis=0, keepdims=True)
        else:
            blk_max = bm_ref[h]
        m_prev = m_sc[h]
        m_new = jnp.maximum(m_prev, blk_max)
        alpha = jnp.exp2(m_prev - m_new)
        p_t = jnp.exp2(s_t - m_new).astype(BF16)
        v_t = vt_ref[j, h * V_ROWS:(h + 1) * V_ROWS, :]
        acc_sc[h] = alpha * acc_sc[h] + _dot(v_t, p_t)
        m_sc[h] = m_new

    def step(j, cur, nxt):
        for h in range(HEADS):
            if nxt is not None:
                produce(j + 1, h, *nxt)
            accumulate(j, h, *cur, nxt is None)

    buf_a = (sa_sc, ma_sc)
    buf_b = (sb_sc, mb_sc)

    def pair(jj, carry):
        step(2 * jj, buf_a, buf_b)
        step(2 * jj + 1, buf_b, buf_a)
        return carry

    for h in range(HEADS):
        produce(0, h, *buf_a)
    lax.fori_loop(0, i // 2, pair, 0)

    @pl.when(i % 2 == 0)
    def _():
        step(i, buf_a, None)

    @pl.when(i % 2 == 1)
    def _():
        step(i - 1, buf_a, buf_b)
        step(i, buf_b, None)

    outs = []
    for h in range(HEADS):
        acc = acc_sc[h]
        outs.append(acc[:VDIM] / acc[VDIM:VDIM + 1])
    o_ref[...] = jnp.concatenate(outs, axis=0).T.astype(BF16)


def _attention(q_t, k, v_t, batch, seq):
    nq = seq // T_ATT
    cols = HEADS * HEAD_LANES
    return pl.pallas_call(
        _attn_kernel,
        grid=(batch, nq),
        in_specs=[pl.BlockSpec((None, cols, T_ATT), lambda b, i: (b * nq + i, 0, 0)),
                  pl.BlockSpec((seq, cols), lambda b, i: (b, 0)),
                  pl.BlockSpec((nq, HEADS * V_ROWS, T_ATT), lambda b, i: (b, 0, 0))],
        out_specs=pl.BlockSpec((T_ATT, HEADS * VDIM), lambda b, i: (b * nq + i, 0)),
        out_shape=jax.ShapeDtypeStruct((batch * seq, HEADS * VDIM), BF16),
        scratch_shapes=[pltpu.VMEM((HEADS, 1, T_ATT), F32),
                        pltpu.VMEM((HEADS, V_ROWS, T_ATT), F32),
                        pltpu.VMEM((HEADS, T_ATT, T_ATT), F32),
                        pltpu.VMEM((HEADS, T_ATT, T_ATT), F32),
                        pltpu.VMEM((HEADS, 1, T_ATT), F32),
                        pltpu.VMEM((HEADS, 1, T_ATT), F32)],
        compiler_params=pltpu.CompilerParams(dimension_semantics=("arbitrary", "arbitrary"),
                                             vmem_limit_bytes=VMEM_LIMIT),
        name="attention",
    )(q_t, k, v_t)


def _mixer_kernel(tiles_per_seq, x_ref, o_ref, g_ref, wb_ref, gb_ref, lng_ref, lnb_ref, wsg_ref, sgb_ref,
                  cw_ref, wpool_ref, pscale_ref, wbr_a_ref, wbr_b_ref, wbr_c_ref, wbr_d_ref, wout_ref, gpost_ref,
                  out_ref, ez_ref, ep_ref):
    t = x_ref.shape[0]
    i = pl.program_id(0)
    seq_tile = lax.rem(i, tiles_per_seq)

    @pl.when(seq_tile == 0)
    def _():
        ez_ref[0:HALO, :] = jnp.zeros((HALO, BR_WIDTH), F32)
        ep_ref[0:HALO, :] = jnp.zeros((HALO, BR_WIDTH), F32)

    x = x_ref[...]
    h = _rms(x, g_ref[...]).astype(BF16)
    o0 = 2 * SG_WIDTH
    o1 = o0 + 3 * BR_WIDTH
    o2 = o1 + BR_WIDTH

    def gate_pre(br):
        return (_dot(h, wb_ref[:, o2 + br * D_MODEL:o2 + (br + 1) * D_MODEL])
                + gb_ref[:, br * D_MODEL:(br + 1) * D_MODEL])

    def gated(gp, y):
        return (1.0 + jnp.tanh(gp)) * y

    uv = _dot(h, wb_ref[:, 0:2 * SG_WIDTH])
    gp_b = gate_pre(1)
    u = jax.nn.gelu(uv[:, :SG_WIDTH])
    gv = jax.nn.gelu(uv[:, SG_WIDTH:])
    mu = jnp.mean(gv, axis=-1, keepdims=True)
    gc = gv - mu
    vln = gc * lax.rsqrt(jnp.mean(gc * gc, axis=-1, keepdims=True) + EPS) * lng_ref[...] + lnb_ref[...]
    vb = vln.astype(BF16)
    trow = lax.broadcasted_iota(jnp.int32, (SG_CHUNK, SG_GROUPS * SG_CHUNK), 0)
    scol = lax.rem(lax.broadcasted_iota(jnp.int32, (SG_CHUNK, SG_GROUPS * SG_CHUNK), 1), SG_CHUNK)
    wsg = jnp.where(scol <= trow, wsg_ref[...], jnp.zeros((), BF16))
    lane_group = lax.broadcasted_iota(jnp.int32, (SG_CHUNK, SG_WIDTH), 1) // (SG_WIDTH // SG_GROUPS)
    mixed_chunks = []
    for c in range(t // SG_CHUNK):
        vc = vb[c * SG_CHUNK:(c + 1) * SG_CHUNK]
        rhs = jnp.concatenate([jnp.where(lane_group == g, vc, jnp.zeros((), BF16))
                               for g in range(SG_GROUPS)], axis=0)
        mixed_chunks.append(_dot(wsg, rhs) + sgb_ref[...])

    cv = _dot(h, wb_ref[:, o0:o0 + 3 * BR_WIDTH])
    gp_c = gate_pre(2)
    y_b = _dot((u * jnp.concatenate(mixed_chunks, axis=0)).astype(BF16), wbr_b_ref[...])
    merged = gated(gp_b, y_b)
    z = cv[:, 2 * BR_WIDTH:] * cv[:, :BR_WIDTH]
    ez_ref[HALO:HALO + t, :] = z
    conv = (cw_ref[0:1, :] * ez_ref[HALO - 2:HALO - 2 + t, :]
            + cw_ref[1:2, :] * ez_ref[HALO - 1:HALO - 1 + t, :]
            + cw_ref[2:3, :] * z)
    ez_ref[0:HALO, :] = ez_ref[t:t + HALO, :]

    ep_ref[HALO:HALO + t, :] = _dot(h, wb_ref[:, o1:o1 + BR_WIDTH])
    gp_d = gate_pre(3)
    y_c = _dot((cv[:, BR_WIDTH:2 * BR_WIDTH] * conv).astype(BF16), wbr_c_ref[...])
    merged = merged + gated(gp_c, y_c)
    tpos = seq_tile * t + lax.broadcasted_iota(jnp.int32, (t, 128), 0) + 1
    lane_lo = lax.broadcasted_iota(jnp.int32, (t, 128), 1) < 64
    pooled = []
    for half, (w_lo, w_hi) in enumerate(((POOL_WINDOWS[0], POOL_WINDOWS[1]),
                                          (POOL_WINDOWS[2], POOL_WINDOWS[3]))):
        cols = slice(half * 128, (half + 1) * 128)
        cur = ep_ref[HALO:HALO + t, cols]
        run = cur
        for d in range(1, w_lo):
            run = run + ep_ref[HALO - d:HALO - d + t, cols]
        sum_lo = run
        for d in range(w_lo, w_hi):
            run = run + ep_ref[HALO - d:HALO - d + t, cols]
        win = jnp.where(lane_lo, w_lo, w_hi)
        cnt = jnp.minimum(tpos, win).astype(F32)
        pooled.append(jnp.where(lane_lo, sum_lo, run) / cnt - cur)
    pooled = jnp.concatenate(pooled, axis=1).astype(BF16)
    ep_ref[0:HALO, :] = ep_ref[t:t + HALO, :]
    mixed_d = _dot(pooled, wpool_ref[...]) * pscale_ref[...]
    gp_a = gate_pre(0)
    y_d = _dot(mixed_d.astype(BF16), wbr_d_ref[...])
    merged = merged + gated(gp_d, y_d)

    y_a = _dot(o_ref[...], wbr_a_ref[...])
    half = t // 2
    for r in range(2):
        rows = slice(r * half, (r + 1) * half)
        m_r = merged[rows] + gated(gp_a[rows], y_a[rows])
        mo = _dot(m_r.astype(BF16), wout_ref[...])
        out_ref[rows, :] = x[rows] + _rms(mo, gpost_ref[...])


def _mixer(l, x2, o, w, seq):
    n = x2.shape[0]
    t = T_MIX
    return pl.pallas_call(
        functools.partial(_mixer_kernel, seq // t),
        grid=(n // t,),
        in_specs=[_row_spec(t, D_MODEL),
                  _row_spec(t, HEADS * VDIM),
                  _layer_spec(l, 1, D_MODEL),
                  _layer_spec(l, D_MODEL, REST_COLS),
                  _layer_spec(l, 1, N_BRANCH * D_MODEL),
                  _layer_spec(l, 1, SG_WIDTH),
                  _layer_spec(l, 1, SG_WIDTH),
                  _layer_spec(l, SG_CHUNK, SG_GROUPS * SG_CHUNK),
                  _layer_spec(l, SG_CHUNK, SG_WIDTH),
                  _layer_spec(l, 3, BR_WIDTH),
                  _layer_spec(l, BR_WIDTH, BR_WIDTH),
                  _layer_spec(l, 1, BR_WIDTH),
                  _layer_spec(l, BR_WIDTH, D_MODEL),
                  _layer_spec(l, BR_WIDTH, D_MODEL),
                  _layer_spec(l, BR_WIDTH, D_MODEL),
                  _layer_spec(l, BR_WIDTH, D_MODEL),
                  _layer_spec(l, D_MODEL, D_MODEL),
                  _layer_spec(l, 1, D_MODEL)],
        out_specs=_row_spec(t, D_MODEL),
        out_shape=jax.ShapeDtypeStruct((n, D_MODEL), F32),
        scratch_shapes=[pltpu.VMEM((t + HALO, BR_WIDTH), F32),
                        pltpu.VMEM((t + HALO, BR_WIDTH), F32)],
        compiler_params=_params(),
        name="mixer",
    )(x2, o, w["norm_mix_pre"], w["w_rest_in"], w["gate_b"], w["sg_ln_g"], w["sg_ln_b"], w["w_sg"],
      w["sg_bias"], w["conv_w"], w["w_pool"], w["pool_scale"], w["w_br_mla"], w["w_br_sg"],
      w["w_br_conv"], w["w_br_pool"], w["w_out"], w["norm_mix_post"])


def _ffn_kernel(x_ref, g_ref, w1_ref, w2_ref, gpost_ref, *rest):
    fused = len(rest) > 1
    out_ref = rest[-4] if fused else rest[0]
    x = x_ref[...]
    h = _rms(x, g_ref[...]).astype(BF16)
    a = jnp.maximum(_dot(h, w1_ref[...]), 0.0)
    f = _dot((a * a).astype(BF16), w2_ref[...])
    x_new = x + _rms(f, gpost_ref[...])
    out_ref[...] = x_new
    if fused:
        _attn_proj_body(x_new, slice(None), *rest[:-4], *rest[-3:])


def _ffn(l, x2, w, tables=None):
    n = x2.shape[0]
    in_specs = [_row_spec(T_FFN, D_MODEL),
                _layer_spec(l, 1, D_MODEL),
                _layer_spec(l, D_MODEL, D_FF),
                _layer_spec(l, D_FF, D_MODEL),
                _layer_spec(l, 1, D_MODEL)]
    operands = [x2, w["norm_ffn_pre"], w["w_ff1"], w["w_ff2"], w["norm_ffn_post"]]
    out_specs = [_row_spec(T_FFN, D_MODEL)]
    out_shape = [jax.ShapeDtypeStruct((n, D_MODEL), F32)]
    if tables is not None:
        p_in, p_ops, p_out, p_shape = _attn_proj_io(l + 1, n, tables, w)
        in_specs += p_in
        operands += p_ops
        out_specs += p_out
        out_shape += p_shape
    return pl.pallas_call(
        _ffn_kernel,
        grid=(n // T_FFN,),
        in_specs=in_specs,
        out_specs=out_specs,
        out_shape=out_shape,
        compiler_params=_params(),
        name="ffn" if tables is None else "ffn_proj",
    )(*operands)


def _gather_cols(w, idx, sign):
    return jnp.take(w, jnp.asarray(idx), axis=-1) * jnp.asarray(sign, F32)


def _w_attn_layout_kernel(wt_ref, wa_ref):
    w = wt_ref[...].T
    r0 = Q_RANK + KV_RANK
    blk = w[:, r0:r0 + HEAD_LANES]
    lane = lax.broadcasted_iota(jnp.int32, blk.shape, 1)
    half = ROPE // 2
    lo = NOPE + ROPE
    rope = jnp.where((lane >= NOPE) & (lane < lo), pltpu.roll(blk, NOPE, 1),
                     jnp.where((lane >= lo) & (lane < lo + half), -pltpu.roll(blk, lo - half, 1),
                               jnp.where(lane >= lo + half, pltpu.roll(blk, lo + half, 1), 0.0)))
    wa_ref[...] = jnp.concatenate([w[:, :r0], rope], axis=1).astype(BF16)


def _w_rest_layout_kernel(wt_ref, wb_ref):
    scale = jnp.where(pl.program_id(1) >= GATE_COLS0 // W_COL_BLOCK, 0.5, 1.0)
    wb_ref[...] = (wt_ref[0] * scale).T.astype(BF16)


def _w_in_layout(w_in):
    w_t = jnp.swapaxes(w_in, 1, 2)
    params = pltpu.CompilerParams(dimension_semantics=("arbitrary", "arbitrary"), vmem_limit_bytes=VMEM_LIMIT)
    w_attn = pl.pallas_call(
        _w_attn_layout_kernel,
        grid=(DEPTH, 1),
        in_specs=[pl.BlockSpec((None, 4 * HEAD_LANES, D_MODEL), lambda l, c: (l, 0, 0))],
        out_specs=pl.BlockSpec((None, D_MODEL, ATTN_IN_COLS), lambda l, c: (l, 0, 0)),
        out_shape=jax.ShapeDtypeStruct((DEPTH, D_MODEL, ATTN_IN_COLS), BF16),
        compiler_params=params,
        name="w_attn_layout",
    )(w_t)
    w_rest = pl.pallas_call(
        _w_rest_layout_kernel,
        grid=(DEPTH, REST_COLS // W_COL_BLOCK),
        in_specs=[pl.BlockSpec((pl.Element(1), pl.Element(W_COL_BLOCK), pl.Element(D_MODEL)),
                               lambda l, c: (l, pl.multiple_of(ATTN_COLS + c * W_COL_BLOCK, 32), 0))],
        out_specs=pl.BlockSpec((None, D_MODEL, W_COL_BLOCK), lambda l, c: (l, 0, c)),
        out_shape=jax.ShapeDtypeStruct((DEPTH, D_MODEL, REST_COLS), BF16),
        compiler_params=params,
        name="w_rest_layout",
    )(w_t)
    return w_attn, w_rest


def _q_layout():
    nq = HEADS * HEAD_LANES
    idx = np.zeros(nq + HEADS * ROPE, np.int32)
    sign = np.zeros(nq + HEADS * ROPE, np.float32)
    half = ROPE // 2
    for h in range(HEADS):
        src = h * (NOPE + ROPE)
        dst = h * HEAD_LANES
        for j in range(NOPE + ROPE):
            idx[dst + j] = src + j
            sign[dst + j] = 1.0
        rot = nq + h * ROPE
        for j in range(half):
            idx[rot + j] = src + NOPE + half + j
            sign[rot + j] = -1.0
            idx[rot + half + j] = src + NOPE + j
            sign[rot + half + j] = 1.0
    return idx, sign


def _k_layout():
    nq = HEADS * HEAD_LANES
    idx = np.zeros(nq, np.int32)
    sign = np.zeros(nq, np.float32)
    for h in range(HEADS):
        src = h * (NOPE + VDIM)
        for j in range(NOPE):
            idx[h * HEAD_LANES + j] = src + j
            sign[h * HEAD_LANES + j] = 1.0
    return idx, sign


def _v_layout():
    idx = np.zeros(HEADS * VDIM, np.int32)
    for h in range(HEADS):
        for j in range(VDIM):
            idx[h * VDIM + j] = h * (NOPE + VDIM) + NOPE + j
    return idx, np.ones(HEADS * VDIM, np.float32)


def _prepare_weights(p):
    L = DEPTH
    row = lambda a: a.reshape(L, 1, a.shape[-1])
    eye = jnp.eye(len(POOL_WINDOWS), dtype=F32)
    w_attn_in, w_rest_in = _w_in_layout(p["w_in"])
    return {
        "norm_mix_pre": row(p["norm_mix_pre"]),
        "w_attn_in": w_attn_in,
        "q_norm": row(p["q_norm"]),
        "w_q_t": jnp.swapaxes(_gather_cols(p["w_uq"], *_q_layout()), 1, 2).astype(BF16),
        "kv_norm": row(p["kv_norm"]),
        "w_k": _gather_cols(p["w_ukv"], *_k_layout()).astype(BF16),
        "w_v_t": jnp.swapaxes(_gather_cols(p["w_ukv"], *_v_layout()), 1, 2).astype(BF16),
        "w_rest_in": w_rest_in,
        "gate_b": row(p["gate_b"]) * 0.5,
        "sg_ln_g": row(p["sg_ln_g"]),
        "sg_ln_b": row(p["sg_ln_b"]),
        "w_sg": jnp.transpose(p["sg_w"], (0, 2, 1, 3)).reshape(L, SG_CHUNK, SG_GROUPS * SG_CHUNK).astype(BF16),
        "sg_bias": jnp.repeat(jnp.transpose(p["sg_b"], (0, 2, 1)), SG_WIDTH // SG_GROUPS, axis=-1),
        "conv_w": p["conv_w"],
        "w_pool": jnp.einsum("lgcd,gh->lgchd", p["pool_w"], eye).reshape(L, BR_WIDTH, BR_WIDTH).astype(BF16),
        "pool_scale": row(p["pool_scale"]),
        "w_br_mla": (p["w_br_mla"] * 0.5).astype(BF16),
        "w_br_sg": (p["w_br_sg"] * 0.5).astype(BF16),
        "w_br_conv": (p["w_br_conv"] * 0.5).astype(BF16),
        "w_br_pool": (p["w_br_pool"] * 0.5).astype(BF16),
        "w_out": p["w_out"].astype(BF16),
        "norm_mix_post": row(p["norm_mix_post"]),
        "norm_ffn_pre": row(p["norm_ffn_pre"]),
        "w_ff1": p["w_ff1"].astype(BF16),
        "w_ff2": p["w_ff2"].astype(BF16),
        "norm_ffn_post": row(p["norm_ffn_post"]),
    }


def kernel(x, positions, norm_mix_pre, w_in, gate_b, q_norm, w_uq, kv_norm, w_ukv, w_br_mla, sg_ln_g, sg_ln_b, sg_w, sg_b, w_br_sg, conv_w, w_br_conv, pool_w, pool_scale, w_br_pool, w_out, norm_mix_post, norm_ffn_pre, w_ff1, w_ff2, norm_ffn_post):
    batch, seq, d = x.shape
    w = _prepare_weights(dict(
        norm_mix_pre=norm_mix_pre, w_in=w_in, gate_b=gate_b, q_norm=q_norm, w_uq=w_uq, kv_norm=kv_norm,
        w_ukv=w_ukv, w_br_mla=w_br_mla, sg_ln_g=sg_ln_g, sg_ln_b=sg_ln_b, sg_w=sg_w, sg_b=sg_b,
        w_br_sg=w_br_sg, conv_w=conv_w, w_br_conv=w_br_conv, pool_w=pool_w, pool_scale=pool_scale,
        w_br_pool=w_br_pool, w_out=w_out, norm_mix_post=norm_mix_post, norm_ffn_pre=norm_ffn_pre,
        w_ff1=w_ff1, w_ff2=w_ff2, norm_ffn_post=norm_ffn_post))
    tables = _rope_tables(positions)
    x2 = x.reshape(batch * seq, d)
    assert T_FFN == T_ATT
    q_t, k, v_t = _attn_proj(0, x2, tables, w)
    for l in range(DEPTH):
        o = _attention(q_t, k, v_t, batch, seq)
        x2 = _mixer(l, x2, o, w, seq)
        if l + 1 < DEPTH:
            x2, q_t, k, v_t = _ffn(l, x2, w, tables)
        else:
            (x2,) = _ffn(l, x2, w)
    return x2.reshape(batch, seq, d)
```

```python
import functools

import numpy as np
import jax
import jax.numpy as jnp
from jax import lax
from jax.experimental import pallas as pl
from jax.experimental.pallas import tpu as pltpu

D_MODEL = 1024
DEPTH = 4
HEADS = 4
NOPE = 64
ROPE = 32
VDIM = 64
Q_RANK = 256
KV_RANK = 128
ROPE_BASE = 10000.0
SG_WIDTH = 256
SG_GROUPS = 4
SG_CHUNK = 128
BR_WIDTH = 256
POOL_WINDOWS = (2, 4, 8, 16)
N_BRANCH = 4
D_FF = 4 * D_MODEL
EPS = 1e-6
HALO = 16
HEAD_LANES = 128
ATTN_COLS = Q_RANK + KV_RANK + ROPE
ATTN_IN_COLS = Q_RANK + KV_RANK + HEAD_LANES
GATE_COLS0 = 2 * SG_WIDTH + 3 * BR_WIDTH + BR_WIDTH
REST_COLS = GATE_COLS0 + N_BRANCH * D_MODEL
W_COL_BLOCK = 512
assert GATE_COLS0 % W_COL_BLOCK == 0 and REST_COLS % W_COL_BLOCK == 0

V_ROWS = VDIM + 16
Q_SCALE = (NOPE + ROPE) ** -0.5 * 1.4426950408889634

T_ROPE = 1024
T_ATT = 512
UNROLL = 4
assert UNROLL % 2 == 0
T_MIX = 512
T_FFN = 512

VMEM_LIMIT = 56 * 1024 * 1024

BF16 = jnp.bfloat16
F32 = jnp.float32


def _dot(a, b):
    return jnp.dot(a, b, preferred_element_type=F32)


def _dot_nt(a, b):
    return lax.dot_general(a, b, (((1,), (1,)), ((), ())), preferred_element_type=F32)


def _rms(x, g):
    return x * lax.rsqrt(jnp.mean(x * x, axis=-1, keepdims=True) + EPS) * g


def _params():
    return pltpu.CompilerParams(dimension_semantics=("arbitrary",), vmem_limit_bytes=VMEM_LIMIT)


def _layer_spec(l, *tail):
    zeros = (0,) * len(tail)
    return pl.BlockSpec((None,) + tuple(tail), lambda *_: (l,) + zeros,
                        pipeline_mode=pl.Buffered(1))


def _row_spec(t, n):
    return pl.BlockSpec((t, n), lambda i: (i, 0))


def _rope_table_kernel(pos_ref, freq_ref, cos_ref, sin_ref, cos_t_ref, sin_t_ref):
    t = pos_ref.shape[1]
    ang = freq_ref[...] * pos_ref[...].astype(F32)
    ones = lambda n: jnp.ones((n, t), F32)
    zeros = lambda n: jnp.zeros((n, t), F32)
    pad = HEAD_LANES - NOPE - ROPE
    cos_t = jnp.concatenate([ones(NOPE), jnp.cos(ang), ones(pad)], axis=0)
    sin_t = jnp.concatenate([zeros(NOPE), jnp.sin(ang), zeros(pad)], axis=0)
    cos_t_ref[...] = cos_t
    sin_t_ref[...] = sin_t
    cos_ref[...] = cos_t.T
    sin_ref[...] = sin_t.T


def _rope_tables(positions):
    n = positions.size
    inv_freq = ROPE_BASE ** (-jnp.arange(0, ROPE, 2, dtype=F32) / ROPE)
    freq = jnp.concatenate([inv_freq, inv_freq]).reshape(ROPE, 1)
    return pl.pallas_call(
        _rope_table_kernel,
        grid=(n // T_ROPE,),
        in_specs=[pl.BlockSpec((1, T_ROPE), lambda i: (0, i)),
                  pl.BlockSpec((ROPE, 1), lambda i: (0, 0))],
        out_specs=[_row_spec(T_ROPE, HEAD_LANES), _row_spec(T_ROPE, HEAD_LANES),
                   pl.BlockSpec((HEAD_LANES, T_ROPE), lambda i: (0, i)),
                   pl.BlockSpec((HEAD_LANES, T_ROPE), lambda i: (0, i))],
        out_shape=[jax.ShapeDtypeStruct((n, HEAD_LANES), F32)] * 2
        + [jax.ShapeDtypeStruct((HEAD_LANES, n), F32)] * 2,
        compiler_params=_params(),
        name="rope_tables",
    )(positions.reshape(1, n), freq)


def _attn_proj_body(x, rows, g_ref, wa_ref, qn_ref, wqt_ref, kvn_ref, wk_ref, wvt_ref,
                    cos_ref, sin_ref, cos_t_ref, sin_t_ref, qt_ref, k_ref, vt_ref):
    t = x.shape[0]
    h = _rms(x, g_ref[...]).astype(BF16)
    pa = _dot(h, wa_ref[...])
    c_q = pa[:, :Q_RANK]
    c_kv = pa[:, Q_RANK:Q_RANK + KV_RANK]
    kr = pa[:, Q_RANK + KV_RANK:]
    nq = HEADS * HEAD_LANES

    qlr_t = _dot_nt(wqt_ref[...], _rms(c_q, qn_ref[...]).astype(BF16))
    cos_t = cos_t_ref[:, rows]
    sin_rope_t = sin_t_ref[NOPE:NOPE + ROPE, rows]
    zeros = lambda n: jnp.zeros((n, t), F32)
    q_heads = []
    for hd in range(HEADS):
        rot = qlr_t[nq + hd * ROPE:nq + (hd + 1) * ROPE] * sin_rope_t
        q_heads.append(qlr_t[hd * HEAD_LANES:(hd + 1) * HEAD_LANES] * cos_t
                       + jnp.concatenate([zeros(NOPE), rot, zeros(HEAD_LANES - NOPE - ROPE)], axis=0))
    qt_ref[:, rows] = (jnp.concatenate(q_heads, axis=0) * Q_SCALE).astype(BF16)

    kvn = _rms(c_kv, kvn_ref[...]).astype(BF16)
    lane = lax.broadcasted_iota(jnp.int32, kr.shape, 1)
    k_rope = jnp.where(lane < NOPE + ROPE,
                       kr * cos_ref[rows, :] + pltpu.roll(kr, HEAD_LANES - ROPE, 1) * sin_ref[rows, :],
                       0.0)
    k = _dot(kvn, wk_ref[...]) + jnp.concatenate([k_rope] * HEADS, axis=1)
    k_ref[rows, :] = k.astype(BF16)

    v_t = _dot_nt(wvt_ref[...], kvn)
    ones_rows = (lax.broadcasted_iota(jnp.int32, (V_ROWS - VDIM, t), 0) == 0).astype(F32)
    pieces = []
    for hd in range(HEADS):
        pieces += [v_t[hd * VDIM:(hd + 1) * VDIM], ones_rows]
    vt_ref[:, rows] = jnp.concatenate(pieces, axis=0).astype(BF16)


def _attn_proj_kernel(x_ref, *refs):
    _attn_proj_body(x_ref[...], slice(None), *refs)


def _attn_proj_io(l, n, tables, w):
    nq = HEADS * HEAD_LANES
    col_spec = pl.BlockSpec((HEAD_LANES, T_ATT), lambda i: (0, i))
    in_specs = [_layer_spec(l, 1, D_MODEL),
                _layer_spec(l, D_MODEL, ATTN_IN_COLS),
                _layer_spec(l, 1, Q_RANK),
                _layer_spec(l, nq + HEADS * ROPE, Q_RANK),
                _layer_spec(l, 1, KV_RANK),
                _layer_spec(l, KV_RANK, nq),
                _layer_spec(l, HEADS * VDIM, KV_RANK),
                _row_spec(T_ATT, HEAD_LANES),
                _row_spec(T_ATT, HEAD_LANES),
                col_spec, col_spec]
    operands = [w["norm_mix_pre"], w["w_attn_in"], w["q_norm"], w["w_q_t"], w["kv_norm"], w["w_k"],
                w["w_v_t"], *tables]
    out_specs = [pl.BlockSpec((None, nq, T_ATT), lambda i: (i, 0, 0)),
                 _row_spec(T_ATT, nq),
                 pl.BlockSpec((None, HEADS * V_ROWS, T_ATT), lambda i: (i, 0, 0))]
    out_shape = [jax.ShapeDtypeStruct((n // T_ATT, nq, T_ATT), BF16),
                 jax.ShapeDtypeStruct((n, nq), BF16),
                 jax.ShapeDtypeStruct((n // T_ATT, HEADS * V_ROWS, T_ATT), BF16)]
    return in_specs, operands, out_specs, out_shape


def _attn_proj(l, x2, tables, w):
    n = x2.shape[0]
    in_specs, operands, out_specs, out_shape = _attn_proj_io(l, n, tables, w)
    return pl.pallas_call(
        _attn_proj_kernel,
        grid=(n // T_ATT,),
        in_specs=[_row_spec(T_ATT, D_MODEL)] + in_specs,
        out_specs=out_specs,
        out_shape=out_shape,
        compiler_params=_params(),
        name="attn_proj",
    )(x2, *operands)


def _attn_kernel(qt_ref, k_ref, vt_ref, o_ref, m_sc, acc_sc, sa_sc, sb_sc, ma_sc, mb_sc):
    i = pl.program_id(1)
    m_sc[...] = jnp.full(m_sc.shape, -1e30, F32)
    acc_sc[...] = jnp.zeros(acc_sc.shape, F32)

    def produce(j, h, s_ref, bm_ref):
        off = pl.multiple_of(j * T_ATT, T_ATT)
        k = k_ref[pl.ds(off, T_ATT), h * HEAD_LANES:(h + 1) * HEAD_LANES]
        s_t = _dot(k, qt_ref[h * HEAD_LANES:(h + 1) * HEAD_LANES, :])
        s_ref[h] = s_t
        bm_ref[h] = jnp.max(s_t, axis=0, keepdims=True)

    def accumulate(j, h, s_ref, bm_ref, masked):
        s_t = s_ref[h]
        if masked:
            key = lax.broadcasted_iota(jnp.int32, s_t.shape, 0)
            qry = lax.broadcasted_iota(jnp.int32, s_t.shape, 1)
            s_t = jnp.where(key <= qry, s_t, -jnp.inf)
            blk_max = jnp.max(s_t, axis=0, keepdims=True)
        else:
            blk_max = bm_ref[h]
        m_prev = m_sc[h]
        m_new = jnp.maximum(m_prev, blk_max)
        alpha = jnp.exp2(m_prev - m_new)
        p_t = jnp.exp2(s_t - m_new).astype(BF16)
        v_t = vt_ref[j, h * V_ROWS:(h + 1) * V_ROWS, :]
        acc_sc[h] = alpha * acc_sc[h] + _dot(v_t, p_t)
        m_sc[h] = m_new

    def step(j, cur, nxt):
        for h in range(HEADS):
            if nxt is not None:
                produce(j + 1, h, *nxt)
            accumulate(j, h, *cur, nxt is None)

    buf_a = (sa_sc, ma_sc)
    buf_b = (sb_sc, mb_sc)

    def chain(first, count, last_is_diagonal):
        bufs = (buf_a, buf_b)
        for n in range(count):
            diagonal = last_is_diagonal and n == count - 1
            step(first + n, bufs[n % 2], None if diagonal else bufs[(n + 1) % 2])

    def trip(jj, carry):
        chain(UNROLL * jj, UNROLL, False)
        return carry

    for h in range(HEADS):
        produce(0, h, *buf_a)
    lax.fori_loop(0, i // UNROLL, trip, 0)
    base = (i // UNROLL) * UNROLL
    for rem in range(UNROLL):
        @pl.when(i - base == rem)
        def _(rem=rem):
            chain(base, rem + 1, True)

    outs = []
    for h in range(HEADS):
        acc = acc_sc[h]
        outs.append(acc[:VDIM] / acc[VDIM:VDIM + 1])
    o_ref[...] = jnp.concatenate(outs, axis=0).T.astype(BF16)


def _attention(q_t, k, v_t, batch, seq):
    nq = seq // T_ATT
    cols = HEADS * HEAD_LANES
    return pl.pallas_call(
        _attn_kernel,
        grid=(batch, nq),
        in_specs=[pl.BlockSpec((None, cols, T_ATT), lambda b, i: (b * nq + i, 0, 0)),
                  pl.BlockSpec((seq, cols), lambda b, i: (b, 0)),
                  pl.BlockSpec((nq, HEADS * V_ROWS, T_ATT), lambda b, i: (b, 0, 0))],
        out_specs=pl.BlockSpec((T_ATT, HEADS * VDIM), lambda b, i: (b * nq + i, 0)),
        out_shape=jax.ShapeDtypeStruct((batch * seq, HEADS * VDIM), BF16),
        scratch_shapes=[pltpu.VMEM((HEADS, 1, T_ATT), F32),
                        pltpu.VMEM((HEADS, V_ROWS, T_ATT), F32),
                        pltpu.VMEM((HEADS, T_ATT, T_ATT), F32),
                        pltpu.VMEM((HEADS, T_ATT, T_ATT), F32),
                        pltpu.VMEM((HEADS, 1, T_ATT), F32),
                        pltpu.VMEM((HEADS, 1, T_ATT), F32)],
        compiler_params=pltpu.CompilerParams(dimension_semantics=("arbitrary", "arbitrary"),
                                             vmem_limit_bytes=VMEM_LIMIT),
        name="attention",
    )(q_t, k, v_t)


def _mixer_kernel(tiles_per_seq, x_ref, o_ref, g_ref, wb_ref, gb_ref, lng_ref, lnb_ref, wsg_ref, sgb_ref,
                  cw_ref, wpool_ref, pscale_ref, wbr_a_ref, wbr_b_ref, wbr_c_ref, wbr_d_ref, wout_ref, gpost_ref,
                  out_ref, ez_ref, ep_ref):
    t = x_ref.shape[0]
    i = pl.program_id(0)
    seq_tile = lax.rem(i, tiles_per_seq)

    @pl.when(seq_tile == 0)
    def _():
        ez_ref[0:HALO, :] = jnp.zeros((HALO, BR_WIDTH), F32)
        ep_ref[0:HALO, :] = jnp.zeros((HALO, BR_WIDTH), F32)

    x = x_ref[...]
    h = _rms(x, g_ref[...]).astype(BF16)
    o0 = 2 * SG_WIDTH
    o1 = o0 + 3 * BR_WIDTH
    o2 = o1 + BR_WIDTH

    def gate_pre(br):
        return (_dot(h, wb_ref[:, o2 + br * D_MODEL:o2 + (br + 1) * D_MODEL])
                + gb_ref[:, br * D_MODEL:(br + 1) * D_MODEL])

    def gated(gp, y):
        return (1.0 + jnp.tanh(gp)) * y

    uv = _dot(h, wb_ref[:, 0:2 * SG_WIDTH])
    gp_b = gate_pre(1)
    u = jax.nn.gelu(uv[:, :SG_WIDTH])
    gv = jax.nn.gelu(uv[:, SG_WIDTH:])
    mu = jnp.mean(gv, axis=-1, keepdims=True)
    gc = gv - mu
    vln = gc * lax.rsqrt(jnp.mean(gc * gc, axis=-1, keepdims=True) + EPS) * lng_ref[...] + lnb_ref[...]
    vb = vln.astype(BF16)
    trow = lax.broadcasted_iota(jnp.int32, (SG_CHUNK, SG_GROUPS * SG_CHUNK), 0)
    scol = lax.rem(lax.broadcasted_iota(jnp.int32, (SG_CHUNK, SG_GROUPS * SG_CHUNK), 1), SG_CHUNK)
    wsg = jnp.where(scol <= trow, wsg_ref[...], jnp.zeros((), BF16))
    lane_group = lax.broadcasted_iota(jnp.int32, (SG_CHUNK, SG_WIDTH), 1) // (SG_WIDTH // SG_GROUPS)
    mixed_chunks = []
    for c in range(t // SG_CHUNK):
        vc = vb[c * SG_CHUNK:(c + 1) * SG_CHUNK]
        rhs = jnp.concatenate([jnp.where(lane_group == g, vc, jnp.zeros((), BF16))
                               for g in range(SG_GROUPS)], axis=0)
        mixed_chunks.append(_dot(wsg, rhs) + sgb_ref[...])

    cv = _dot(h, wb_ref[:, o0:o0 + 3 * BR_WIDTH])
    gp_c = gate_pre(2)
    y_b = _dot((u * jnp.concatenate(mixed_chunks, axis=0)).astype(BF16), wbr_b_ref[...])
    merged = gated(gp_b, y_b)
    z = cv[:, 2 * BR_WIDTH:] * cv[:, :BR_WIDTH]
    ez_ref[HALO:HALO + t, :] = z
    conv = (cw_ref[0:1, :] * ez_ref[HALO - 2:HALO - 2 + t, :]
            + cw_ref[1:2, :] * ez_ref[HALO - 1:HALO - 1 + t, :]
            + cw_ref[2:3, :] * z)
    ez_ref[0:HALO, :] = ez_ref[t:t + HALO, :]

    ep_ref[HALO:HALO + t, :] = _dot(h, wb_ref[:, o1:o1 + BR_WIDTH])
    gp_d = gate_pre(3)
    y_c = _dot((cv[:, BR_WIDTH:2 * BR_WIDTH] * conv).astype(BF16), wbr_c_ref[...])
    merged = merged + gated(gp_c, y_c)
    tpos = seq_tile * t + lax.broadcasted_iota(jnp.int32, (t, 128), 0) + 1
    lane_lo = lax.broadcasted_iota(jnp.int32, (t, 128), 1) < 64
    pooled = []
    for half, (w_lo, w_hi) in enumerate(((POOL_WINDOWS[0], POOL_WINDOWS[1]),
                                          (POOL_WINDOWS[2], POOL_WINDOWS[3]))):
        cols = slice(half * 128, (half + 1) * 128)
        cur = ep_ref[HALO:HALO + t, cols]
        run = cur
        for d in range(1, w_lo):
            run = run + ep_ref[HALO - d:HALO - d + t, cols]
        sum_lo = run
        for d in range(w_lo, w_hi):
            run = run + ep_ref[HALO - d:HALO - d + t, cols]
        win = jnp.where(lane_lo, w_lo, w_hi)
        cnt = jnp.minimum(tpos, win).astype(F32)
        pooled.append(jnp.where(lane_lo, sum_lo, run) / cnt - cur)
    pooled = jnp.concatenate(pooled, axis=1).astype(BF16)
    ep_ref[0:HALO, :] = ep_ref[t:t + HALO, :]
    mixed_d = _dot(pooled, wpool_ref[...]) * pscale_ref[...]
    gp_a = gate_pre(0)
    y_d = _dot(mixed_d.astype(BF16), wbr_d_ref[...])
    merged = merged + gated(gp_d, y_d)

    y_a = _dot(o_ref[...], wbr_a_ref[...])
    half = t // 2
    for r in range(2):
        rows = slice(r * half, (r + 1) * half)
        m_r = merged[rows] + gated(gp_a[rows], y_a[rows])
        mo = _dot(m_r.astype(BF16), wout_ref[...])
        out_ref[rows, :] = x[rows] + _rms(mo, gpost_ref[...])


def _mixer(l, x2, o, w, seq):
    n = x2.shape[0]
    t = T_MIX
    return pl.pallas_call(
        functools.partial(_mixer_kernel, seq // t),
        grid=(n // t,),
        in_specs=[_row_spec(t, D_MODEL),
                  _row_spec(t, HEADS * VDIM),
                  _layer_spec(l, 1, D_MODEL),
                  _layer_spec(l, D_MODEL, REST_COLS),
                  _layer_spec(l, 1, N_BRANCH * D_MODEL),
                  _layer_spec(l, 1, SG_WIDTH),
                  _layer_spec(l, 1, SG_WIDTH),
                  _layer_spec(l, SG_CHUNK, SG_GROUPS * SG_CHUNK),
                  _layer_spec(l, SG_CHUNK, SG_WIDTH),
                  _layer_spec(l, 3, BR_WIDTH),
                  _layer_spec(l, BR_WIDTH, BR_WIDTH),
                  _layer_spec(l, 1, BR_WIDTH),
                  _layer_spec(l, BR_WIDTH, D_MODEL),
                  _layer_spec(l, BR_WIDTH, D_MODEL),
                  _layer_spec(l, BR_WIDTH, D_MODEL),
                  _layer_spec(l, BR_WIDTH, D_MODEL),
                  _layer_spec(l, D_MODEL, D_MODEL),
                  _layer_spec(l, 1, D_MODEL)],
        out_specs=_row_spec(t, D_MODEL),
        out_shape=jax.ShapeDtypeStruct((n, D_MODEL), F32),
        scratch_shapes=[pltpu.VMEM((t + HALO, BR_WIDTH), F32),
                        pltpu.VMEM((t + HALO, BR_WIDTH), F32)],
        compiler_params=_params(),
        name="mixer",
    )(x2, o, w["norm_mix_pre"], w["w_rest_in"], w["gate_b"], w["sg_ln_g"], w["sg_ln_b"], w["w_sg"],
      w["sg_bias"], w["conv_w"], w["w_pool"], w["pool_scale"], w["w_br_mla"], w["w_br_sg"],
      w["w_br_conv"], w["w_br_pool"], w["w_out"], w["norm_mix_post"])


def _ffn_kernel(x_ref, g_ref, w1_ref, w2_ref, gpost_ref, *rest):
    fused = len(rest) > 1
    out_ref = rest[-4] if fused else rest[0]
    x = x_ref[...]
    h = _rms(x, g_ref[...]).astype(BF16)
    a = jnp.maximum(_dot(h, w1_ref[...]), 0.0)
    f = _dot((a * a).astype(BF16), w2_ref[...])
    x_new = x + _rms(f, gpost_ref[...])
    out_ref[...] = x_new
    if fused:
        _attn_proj_body(x_new, slice(None), *rest[:-4], *rest[-3:])


def _ffn(l, x2, w, tables=None):
    n = x2.shape[0]
    in_specs = [_row_spec(T_FFN, D_MODEL),
                _layer_spec(l, 1, D_MODEL),
                _layer_spec(l, D_MODEL, D_FF),
                _layer_spec(l, D_FF, D_MODEL),
                _layer_spec(l, 1, D_MODEL)]
    operands = [x2, w["norm_ffn_pre"], w["w_ff1"], w["w_ff2"], w["norm_ffn_post"]]
    out_specs = [_row_spec(T_FFN, D_MODEL)]
    out_shape = [jax.ShapeDtypeStruct((n, D_MODEL), F32)]
    if tables is not None:
        p_in, p_ops, p_out, p_shape = _attn_proj_io(l + 1, n, tables, w)
        in_specs += p_in
        operands += p_ops
        out_specs += p_out
        out_shape += p_shape
    return pl.pallas_call(
        _ffn_kernel,
        grid=(n // T_FFN,),
        in_specs=in_specs,
        out_specs=out_specs,
        out_shape=out_shape,
        compiler_params=_params(),
        name="ffn" if tables is None else "ffn_proj",
    )(*operands)


def _gather_cols(w, idx, sign):
    return jnp.take(w, jnp.asarray(idx), axis=-1) * jnp.asarray(sign, F32)


def _w_attn_layout_kernel(wt_ref, wa_ref):
    w = wt_ref[...].T
    r0 = Q_RANK + KV_RANK
    blk = w[:, r0:r0 + HEAD_LANES]
    lane = lax.broadcasted_iota(jnp.int32, blk.shape, 1)
    half = ROPE // 2
    lo = NOPE + ROPE
    rope = jnp.where((lane >= NOPE) & (lane < lo), pltpu.roll(blk, NOPE, 1),
                     jnp.where((lane >= lo) & (lane < lo + half), -pltpu.roll(blk, lo - half, 1),
                               jnp.where(lane >= lo + half, pltpu.roll(blk, lo + half, 1), 0.0)))
    wa_ref[...] = jnp.concatenate([w[:, :r0], rope], axis=1).astype(BF16)


def _w_rest_layout_kernel(wt_ref, wb_ref):
    scale = jnp.where(pl.program_id(1) >= GATE_COLS0 // W_COL_BLOCK, 0.5, 1.0)
    wb_ref[...] = (wt_ref[0] * scale).T.astype(BF16)


def _w_in_layout(w_in):
    w_t = jnp.swapaxes(w_in, 1, 2)
    params = pltpu.CompilerParams(dimension_semantics=("arbitrary", "arbitrary"), vmem_limit_bytes=VMEM_LIMIT)
    w_attn = pl.pallas_call(
        _w_attn_layout_kernel,
        grid=(DEPTH, 1),
        in_specs=[pl.BlockSpec((None, 4 * HEAD_LANES, D_MODEL), lambda l, c: (l, 0, 0))],
        out_specs=pl.BlockSpec((None, D_MODEL, ATTN_IN_COLS), lambda l, c: (l, 0, 0)),
        out_shape=jax.ShapeDtypeStruct((DEPTH, D_MODEL, ATTN_IN_COLS), BF16),
        compiler_params=params,
        name="w_attn_layout",
    )(w_t)
    w_rest = pl.pallas_call(
        _w_rest_layout_kernel,
        grid=(DEPTH, REST_COLS // W_COL_BLOCK),
        in_specs=[pl.BlockSpec((pl.Element(1), pl.Element(W_COL_BLOCK), pl.Element(D_MODEL)),
                               lambda l, c: (l, pl.multiple_of(ATTN_COLS + c * W_COL_BLOCK, 32), 0))],
        out_specs=pl.BlockSpec((None, D_MODEL, W_COL_BLOCK), lambda l, c: (l, 0, c)),
        out_shape=jax.ShapeDtypeStruct((DEPTH, D_MODEL, REST_COLS), BF16),
        compiler_params=params,
        name="w_rest_layout",
    )(w_t)
    return w_attn, w_rest


def _q_layout():
    nq = HEADS * HEAD_LANES
    idx = np.zeros(nq + HEADS * ROPE, np.int32)
    sign = np.zeros(nq + HEADS * ROPE, np.float32)
    half = ROPE // 2
    for h in range(HEADS):
        src = h * (NOPE + ROPE)
        dst = h * HEAD_LANES
        for j in range(NOPE + ROPE):
            idx[dst + j] = src + j
            sign[dst + j] = 1.0
        rot = nq + h * ROPE
        for j in range(half):
            idx[rot + j] = src + NOPE + half + j
            sign[rot + j] = -1.0
            idx[rot + half + j] = src + NOPE + j
            sign[rot + half + j] = 1.0
    return idx, sign


def _k_layout():
    nq = HEADS * HEAD_LANES
    idx = np.zeros(nq, np.int32)
    sign = np.zeros(nq, np.float32)
    for h in range(HEADS):
        src = h * (NOPE + VDIM)
        for j in range(NOPE):
            idx[h * HEAD_LANES + j] = src + j
            sign[h * HEAD_LANES + j] = 1.0
    return idx, sign


def _v_layout():
    idx = np.zeros(HEADS * VDIM, np.int32)
    for h in range(HEADS):
        for j in range(VDIM):
            idx[h * VDIM + j] = h * (NOPE + VDIM) + NOPE + j
    return idx, np.ones(HEADS * VDIM, np.float32)


def _prepare_weights(p):
    L = DEPTH
    row = lambda a: a.reshape(L, 1, a.shape[-1])
    eye = jnp.eye(len(POOL_WINDOWS), dtype=F32)
    w_attn_in, w_rest_in = _w_in_layout(p["w_in"])
    return {
        "norm_mix_pre": row(p["norm_mix_pre"]),
        "w_attn_in": w_attn_in,
        "q_norm": row(p["q_norm"]),
        "w_q_t": jnp.swapaxes(_gather_cols(p["w_uq"], *_q_layout()), 1, 2).astype(BF16),
        "kv_norm": row(p["kv_norm"]),
        "w_k": _gather_cols(p["w_ukv"], *_k_layout()).astype(BF16),
        "w_v_t": jnp.swapaxes(_gather_cols(p["w_ukv"], *_v_layout()), 1, 2).astype(BF16),
        "w_rest_in": w_rest_in,
        "gate_b": row(p["gate_b"]) * 0.5,
        "sg_ln_g": row(p["sg_ln_g"]),
        "sg_ln_b": row(p["sg_ln_b"]),
        "w_sg": jnp.transpose(p["sg_w"], (0, 2, 1, 3)).reshape(L, SG_CHUNK, SG_GROUPS * SG_CHUNK).astype(BF16),
        "sg_bias": jnp.repeat(jnp.transpose(p["sg_b"], (0, 2, 1)), SG_WIDTH // SG_GROUPS, axis=-1),
        "conv_w": p["conv_w"],
        "w_pool": jnp.einsum("lgcd,gh->lgchd", p["pool_w"], eye).reshape(L, BR_WIDTH, BR_WIDTH).astype(BF16),
        "pool_scale": row(p["pool_scale"]),
        "w_br_mla": (p["w_br_mla"] * 0.5).astype(BF16),
        "w_br_sg": (p["w_br_sg"] * 0.5).astype(BF16),
        "w_br_conv": (p["w_br_conv"] * 0.5).astype(BF16),
        "w_br_pool": (p["w_br_pool"] * 0.5).astype(BF16),
        "w_out": p["w_out"].astype(BF16),
        "norm_mix_post": row(p["norm_mix_post"]),
        "norm_ffn_pre": row(p["norm_ffn_pre"]),
        "w_ff1": p["w_ff1"].astype(BF16),
        "w_ff2": p["w_ff2"].astype(BF16),
        "norm_ffn_post": row(p["norm_ffn_post"]),
    }


def kernel(x, positions, norm_mix_pre, w_in, gate_b, q_norm, w_uq, kv_norm, w_ukv, w_br_mla, sg_ln_g, sg_ln_b, sg_w, sg_b, w_br_sg, conv_w, w_br_conv, pool_w, pool_scale, w_br_pool, w_out, norm_mix_post, norm_ffn_pre, w_ff1, w_ff2, norm_ffn_post):
    batch, seq, d = x.shape
    w = _prepare_weights(dict(
        norm_mix_pre=norm_mix_pre, w_in=w_in, gate_b=gate_b, q_norm=q_norm, w_uq=w_uq, kv_norm=kv_norm,
        w_ukv=w_ukv, w_br_mla=w_br_mla, sg_ln_g=sg_ln_g, sg_ln_b=sg_ln_b, sg_w=sg_w, sg_b=sg_b,
        w_br_sg=w_br_sg, conv_w=conv_w, w_br_conv=w_br_conv, pool_w=pool_w, pool_scale=pool_scale,
        w_br_pool=w_br_pool, w_out=w_out, norm_mix_post=norm_mix_post, norm_ffn_pre=norm_ffn_pre,
        w_ff1=w_ff1, w_ff2=w_ff2, norm_ffn_post=norm_ffn_post))
    tables = _rope_tables(positions)
    x2 = x.reshape(batch * seq, d)
    assert T_FFN == T_ATT
    q_t, k, v_t = _attn_proj(0, x2, tables, w)
    for l in range(DEPTH):
        o = _attention(q_t, k, v_t, batch, seq)
        x2 = _mixer(l, x2, o, w, seq)
        if l + 1 < DEPTH:
            x2, q_t, k, v_t = _ffn(l, x2, w, tables)
        else:
            (x2,) = _ffn(l, x2, w)
    return x2.reshape(batch, seq, d)
```

```python
import functools

import numpy as np
import jax
import jax.numpy as jnp
from jax import lax
from jax.experimental import pallas as pl
from jax.experimental.pallas import tpu as pltpu

D_MODEL = 1024
DEPTH = 4
HEADS = 4
NOPE = 64
ROPE = 32
VDIM = 64
Q_RANK = 256
KV_RANK = 128
ROPE_BASE = 10000.0
SG_WIDTH = 256
SG_GROUPS = 4
SG_CHUNK = 128
BR_WIDTH = 256
POOL_WINDOWS = (2, 4, 8, 16)
N_BRANCH = 4
D_FF = 4 * D_MODEL
EPS = 1e-6
HALO = 16
HEAD_LANES = 128
ATTN_COLS = Q_RANK + KV_RANK + ROPE
ATTN_IN_COLS = Q_RANK + KV_RANK + HEAD_LANES
GATE_COLS0 = 2 * SG_WIDTH + 3 * BR_WIDTH + BR_WIDTH
REST_COLS = GATE_COLS0 + N_BRANCH * D_MODEL
W_COL_BLOCK = 512
assert GATE_COLS0 % W_COL_BLOCK == 0 and REST_COLS % W_COL_BLOCK == 0

V_ROWS = VDIM + 16
Q_SCALE = (NOPE + ROPE) ** -0.5 * 1.4426950408889634

T_ROPE = 1024
T_ATT = 512
UNROLL = 4
assert UNROLL % 2 == 0
T_MIX = 512
T_FFN = 512

VMEM_LIMIT = 56 * 1024 * 1024

BF16 = jnp.bfloat16
F32 = jnp.float32


def _dot(a, b):
    return jnp.dot(a, b, preferred_element_type=F32)


def _dot_nt(a, b):
    return lax.dot_general(a, b, (((1,), (1,)), ((), ())), preferred_element_type=F32)


def _rms(x, g):
    return x * lax.rsqrt(jnp.mean(x * x, axis=-1, keepdims=True) + EPS) * g


def _params():
    return pltpu.CompilerParams(dimension_semantics=("arbitrary",), vmem_limit_bytes=VMEM_LIMIT)


def _layer_spec(l, *tail):
    zeros = (0,) * len(tail)
    return pl.BlockSpec((None,) + tuple(tail), lambda *_: (l,) + zeros,
                        pipeline_mode=pl.Buffered(1))


def _row_spec(t, n):
    return pl.BlockSpec((t, n), lambda i: (i, 0))


def _rope_table_kernel(pos_ref, freq_ref, cos_ref, sin_ref, cos_t_ref, sin_t_ref):
    t = pos_ref.shape[1]
    ang = freq_ref[...] * pos_ref[...].astype(F32)
    ones = lambda n: jnp.ones((n, t), F32)
    zeros = lambda n: jnp.zeros((n, t), F32)
    pad = HEAD_LANES - NOPE - ROPE
    cos_t = jnp.concatenate([ones(NOPE), jnp.cos(ang), ones(pad)], axis=0)
    sin_t = jnp.concatenate([zeros(NOPE), jnp.sin(ang), zeros(pad)], axis=0)
    cos_t_ref[...] = cos_t
    sin_t_ref[...] = sin_t
    cos_ref[...] = cos_t.T
    sin_ref[...] = sin_t.T


def _rope_tables(positions):
    n = positions.size
    inv_freq = ROPE_BASE ** (-jnp.arange(0, ROPE, 2, dtype=F32) / ROPE)
    freq = jnp.concatenate([inv_freq, inv_freq]).reshape(ROPE, 1)
    return pl.pallas_call(
        _rope_table_kernel,
        grid=(n // T_ROPE,),
        in_specs=[pl.BlockSpec((1, T_ROPE), lambda i: (0, i)),
                  pl.BlockSpec((ROPE, 1), lambda i: (0, 0))],
        out_specs=[_row_spec(T_ROPE, HEAD_LANES), _row_spec(T_ROPE, HEAD_LANES),
                   pl.BlockSpec((HEAD_LANES, T_ROPE), lambda i: (0, i)),
                   pl.BlockSpec((HEAD_LANES, T_ROPE), lambda i: (0, i))],
        out_shape=[jax.ShapeDtypeStruct((n, HEAD_LANES), F32)] * 2
        + [jax.ShapeDtypeStruct((HEAD_LANES, n), F32)] * 2,
        compiler_params=_params(),
        name="rope_tables",
    )(positions.reshape(1, n), freq)


def _attn_proj_body(x, rows, g_ref, wa_ref, qn_ref, wqt_ref, kvn_ref, wk_ref, wvt_ref,
                    cos_ref, sin_ref, cos_t_ref, sin_t_ref, qt_ref, k_ref, vt_ref):
    t = x.shape[0]
    h = _rms(x, g_ref[...]).astype(BF16)
    pa = _dot(h, wa_ref[...])
    c_q = pa[:, :Q_RANK]
    c_kv = pa[:, Q_RANK:Q_RANK + KV_RANK]
    kr = pa[:, Q_RANK + KV_RANK:]
    nq = HEADS * HEAD_LANES

    qlr_t = _dot_nt(wqt_ref[...], _rms(c_q, qn_ref[...]).astype(BF16))
    cos_t = cos_t_ref[:, rows]
    sin_rope_t = sin_t_ref[NOPE:NOPE + ROPE, rows]
    zeros = lambda n: jnp.zeros((n, t), F32)
    q_heads = []
    for hd in range(HEADS):
        rot = qlr_t[nq + hd * ROPE:nq + (hd + 1) * ROPE] * sin_rope_t
        q_heads.append(qlr_t[hd * HEAD_LANES:(hd + 1) * HEAD_LANES] * cos_t
                       + jnp.concatenate([zeros(NOPE), rot, zeros(HEAD_LANES - NOPE - ROPE)], axis=0))
    qt_ref[:, rows] = (jnp.concatenate(q_heads, axis=0) * Q_SCALE).astype(BF16)

    kvn = _rms(c_kv, kvn_ref[...]).astype(BF16)
    lane = lax.broadcasted_iota(jnp.int32, kr.shape, 1)
    k_rope = jnp.where(lane < NOPE + ROPE,
                       kr * cos_ref[rows, :] + pltpu.roll(kr, HEAD_LANES - ROPE, 1) * sin_ref[rows, :],
                       0.0)
    k = _dot(kvn, wk_ref[...]) + jnp.concatenate([k_rope] * HEADS, axis=1)
    k_ref[rows, :] = k.astype(BF16)

    v_t = _dot_nt(wvt_ref[...], kvn)
    ones_rows = (lax.broadcasted_iota(jnp.int32, (V_ROWS - VDIM, t), 0) == 0).astype(F32)
    pieces = []
    for hd in range(HEADS):
        pieces += [v_t[hd * VDIM:(hd + 1) * VDIM], ones_rows]
    vt_ref[:, rows] = jnp.concatenate(pieces, axis=0).astype(BF16)


def _attn_proj_kernel(x_ref, *refs):
    _attn_proj_body(x_ref[...], slice(None), *refs)


def _attn_proj_io(l, n, tables, w):
    nq = HEADS * HEAD_LANES
    col_spec = pl.BlockSpec((HEAD_LANES, T_ATT), lambda i: (0, i))
    in_specs = [_layer_spec(l, 1, D_MODEL),
                _layer_spec(l, D_MODEL, ATTN_IN_COLS),
                _layer_spec(l, 1, Q_RANK),
                _layer_spec(l, nq + HEADS * ROPE, Q_RANK),
                _layer_spec(l, 1, KV_RANK),
                _layer_spec(l, KV_RANK, nq),
                _layer_spec(l, HEADS * VDIM, KV_RANK),
                _row_spec(T_ATT, HEAD_LANES),
                _row_spec(T_ATT, HEAD_LANES),
                col_spec, col_spec]
    operands = [w["norm_mix_pre"], w["w_attn_in"], w["q_norm"], w["w_q_t"], w["kv_norm"], w["w_k"],
                w["w_v_t"], *tables]
    out_specs = [pl.BlockSpec((None, nq, T_ATT), lambda i: (i, 0, 0)),
                 _row_spec(T_ATT, nq),
                 pl.BlockSpec((None, HEADS * V_ROWS, T_ATT), lambda i: (i, 0, 0))]
    out_shape = [jax.ShapeDtypeStruct((n // T_ATT, nq, T_ATT), BF16),
                 jax.ShapeDtypeStruct((n, nq), BF16),
                 jax.ShapeDtypeStruct((n // T_ATT, HEADS * V_ROWS, T_ATT), BF16)]
    return in_specs, operands, out_specs, out_shape


def _attn_proj(l, x2, tables, w):
    n = x2.shape[0]
    in_specs, operands, out_specs, out_shape = _attn_proj_io(l, n, tables, w)
    return pl.pallas_call(
        _attn_proj_kernel,
        grid=(n // T_ATT,),
        in_specs=[_row_spec(T_ATT, D_MODEL)] + in_specs,
        out_specs=out_specs,
        out_shape=out_shape,
        compiler_params=_params(),
        name="attn_proj",
    )(x2, *operands)


def _attn_kernel(qt_ref, k_ref, vt_ref, o_ref, m_sc, acc_sc, sa_sc, sb_sc, ma_sc, mb_sc):
    i = pl.program_id(1)
    m_sc[...] = jnp.full(m_sc.shape, -1e30, F32)
    acc_sc[...] = jnp.zeros(acc_sc.shape, F32)

    def produce(j, h, s_ref, bm_ref):
        off = pl.multiple_of(j * T_ATT, T_ATT)
        k = k_ref[pl.ds(off, T_ATT), h * HEAD_LANES:(h + 1) * HEAD_LANES]
        s_t = _dot(k, qt_ref[h * HEAD_LANES:(h + 1) * HEAD_LANES, :])
        s_ref[h] = s_t
        bm_ref[h] = jnp.max(s_t, axis=0, keepdims=True)

    def accumulate(j, h, s_ref, bm_ref, masked):
        s_t = s_ref[h]
        if masked:
            key = lax.broadcasted_iota(jnp.int32, s_t.shape, 0)
            qry = lax.broadcasted_iota(jnp.int32, s_t.shape, 1)
            s_t = jnp.where(key <= qry, s_t, -jnp.inf)
            blk_max = jnp.max(s_t, axis=0, keepdims=True)
        else:
            blk_max = bm_ref[h]
        m_prev = m_sc[h]
        m_new = jnp.maximum(m_prev, blk_max)
        alpha = jnp.exp2(m_prev - m_new)
        p_t = jnp.exp2(s_t - m_new).astype(BF16)
        v_t = vt_ref[j, h * V_ROWS:(h + 1) * V_ROWS, :]
        acc_sc[h] = alpha * acc_sc[h] + _dot(v_t, p_t)
        m_sc[h] = m_new

    def step(j, cur, nxt):
        for h in range(HEADS):
            if nxt is not None:
                produce(j + 1, h, *nxt)
            accumulate(j, h, *cur, nxt is None)

    buf_a = (sa_sc, ma_sc)
    buf_b = (sb_sc, mb_sc)

    def chain(first, count, last_is_diagonal):
        bufs = (buf_a, buf_b)
        for n in range(count):
            diagonal = last_is_diagonal and n == count - 1
            step(first + n, bufs[n % 2], None if diagonal else bufs[(n + 1) % 2])

    def trip(jj, carry):
        chain(UNROLL * jj, UNROLL, False)
        return carry

    for h in range(HEADS):
        produce(0, h, *buf_a)
    lax.fori_loop(0, i // UNROLL, trip, 0)
    base = (i // UNROLL) * UNROLL
    for rem in range(UNROLL):
        @pl.when(i - base == rem)
        def _(rem=rem):
            chain(base, rem + 1, True)

    outs = []
    for h in range(HEADS):
        acc = acc_sc[h]
        outs.append(acc[:VDIM] / acc[VDIM:VDIM + 1])
    o_ref[...] = jnp.concatenate(outs, axis=0).T.astype(BF16)


def _attention(q_t, k, v_t, batch, seq):
    nq = seq // T_ATT
    cols = HEADS * HEAD_LANES
    return pl.pallas_call(
        _attn_kernel,
        grid=(batch, nq),
        in_specs=[pl.BlockSpec((None, cols, T_ATT), lambda b, i: (b * nq + i, 0, 0)),
                  pl.BlockSpec((seq, cols), lambda b, i: (b, 0)),
                  pl.BlockSpec((nq, HEADS * V_ROWS, T_ATT), lambda b, i: (b, 0, 0))],
        out_specs=pl.BlockSpec((T_ATT, HEADS * VDIM), lambda b, i: (b * nq + i, 0)),
        out_shape=jax.ShapeDtypeStruct((batch * seq, HEADS * VDIM), BF16),
        scratch_shapes=[pltpu.VMEM((HEADS, 1, T_ATT), F32),
                        pltpu.VMEM((HEADS, V_ROWS, T_ATT), F32),
                        pltpu.VMEM((HEADS, T_ATT, T_ATT), F32),
                        pltpu.VMEM((HEADS, T_ATT, T_ATT), F32),
                        pltpu.VMEM((HEADS, 1, T_ATT), F32),
                        pltpu.VMEM((HEADS, 1, T_ATT), F32)],
        compiler_params=pltpu.CompilerParams(dimension_semantics=("arbitrary", "arbitrary"),
                                             vmem_limit_bytes=VMEM_LIMIT),
        name="attention",
    )(q_t, k, v_t)


def _mixer_kernel(tiles_per_seq, x_ref, o_ref, g_ref, wb_ref, gb_ref, lng_ref, lnb_ref, wsg_ref, sgb_ref,
                  cw_ref, wpool_ref, pscale_ref, wbr_a_ref, wbr_b_ref, wbr_c_ref, wbr_d_ref, wout_ref, gpost_ref,
                  out_ref, ez_ref, ep_ref):
    t = x_ref.shape[0]
    i = pl.program_id(0)
    seq_tile = lax.rem(i, tiles_per_seq)

    @pl.when(seq_tile == 0)
    def _():
        ez_ref[0:HALO, :] = jnp.zeros((HALO, BR_WIDTH), F32)
        ep_ref[0:HALO, :] = jnp.zeros((HALO, BR_WIDTH), F32)

    x = x_ref[...]
    h = _rms(x, g_ref[...]).astype(BF16)
    o0 = 2 * SG_WIDTH
    o1 = o0 + 3 * BR_WIDTH
    o2 = o1 + BR_WIDTH

    def gate_pre(br):
        return (_dot(h, wb_ref[:, o2 + br * D_MODEL:o2 + (br + 1) * D_MODEL])
                + gb_ref[:, br * D_MODEL:(br + 1) * D_MODEL])

    def gated(gp, y):
        return (1.0 + jnp.tanh(gp)) * y

    uv = _dot(h, wb_ref[:, 0:2 * SG_WIDTH])
    gp_b = gate_pre(1)
    u = jax.nn.gelu(uv[:, :SG_WIDTH])
    gv = jax.nn.gelu(uv[:, SG_WIDTH:])
    mu = jnp.mean(gv, axis=-1, keepdims=True)
    gc = gv - mu
    vln = gc * lax.rsqrt(jnp.mean(gc * gc, axis=-1, keepdims=True) + EPS) * lng_ref[...] + lnb_ref[...]
    vb = vln.astype(BF16)
    trow = lax.broadcasted_iota(jnp.int32, (SG_CHUNK, SG_GROUPS * SG_CHUNK), 0)
    scol = lax.rem(lax.broadcasted_iota(jnp.int32, (SG_CHUNK, SG_GROUPS * SG_CHUNK), 1), SG_CHUNK)
    wsg = jnp.where(scol <= trow, wsg_ref[...], jnp.zeros((), BF16))
    lane_group = lax.broadcasted_iota(jnp.int32, (SG_CHUNK, SG_WIDTH), 1) // (SG_WIDTH // SG_GROUPS)
    mixed_chunks = []
    for c in range(t // SG_CHUNK):
        vc = vb[c * SG_CHUNK:(c + 1) * SG_CHUNK]
        rhs = jnp.concatenate([jnp.where(lane_group == g, vc, jnp.zeros((), BF16))
                               for g in range(SG_GROUPS)], axis=0)
        mixed_chunks.append(_dot(wsg, rhs) + sgb_ref[...])

    cv = _dot(h, wb_ref[:, o0:o0 + 3 * BR_WIDTH])
    gp_c = gate_pre(2)
    y_b = _dot((u * jnp.concatenate(mixed_chunks, axis=0)).astype(BF16), wbr_b_ref[...])
    merged = gated(gp_b, y_b)
    z = cv[:, 2 * BR_WIDTH:] * cv[:, :BR_WIDTH]
    ez_ref[HALO:HALO + t, :] = z
    conv = (cw_ref[0:1, :] * ez_ref[HALO - 2:HALO - 2 + t, :]
            + cw_ref[1:2, :] * ez_ref[HALO - 1:HALO - 1 + t, :]
            + cw_ref[2:3, :] * z)
    ez_ref[0:HALO, :] = ez_ref[t:t + HALO, :]

    ep_ref[HALO:HALO + t, :] = _dot(h, wb_ref[:, o1:o1 + BR_WIDTH])
    gp_d = gate_pre(3)
    y_c = _dot((cv[:, BR_WIDTH:2 * BR_WIDTH] * conv).astype(BF16), wbr_c_ref[...])
    merged = merged + gated(gp_c, y_c)
    tpos = seq_tile * t + lax.broadcasted_iota(jnp.int32, (t, 128), 0) + 1
    lane_lo = lax.broadcasted_iota(jnp.int32, (t, 128), 1) < 64
    pooled = []
    for half, (w_lo, w_hi) in enumerate(((POOL_WINDOWS[0], POOL_WINDOWS[1]),
                                          (POOL_WINDOWS[2], POOL_WINDOWS[3]))):
        cols = slice(half * 128, (half + 1) * 128)
        cur = ep_ref[HALO:HALO + t, cols]
        run = cur
        for d in range(1, w_lo):
            run = run + ep_ref[HALO - d:HALO - d + t, cols]
        sum_lo = run
        for d in range(w_lo, w_hi):
            run = run + ep_ref[HALO - d:HALO - d + t, cols]
        win = jnp.where(lane_lo, w_lo, w_hi)
        cnt = jnp.minimum(tpos, win).astype(F32)
        pooled.append(jnp.where(lane_lo, sum_lo, run) / cnt - cur)
    pooled = jnp.concatenate(pooled, axis=1).astype(BF16)
    ep_ref[0:HALO, :] = ep_ref[t:t + HALO, :]
    mixed_d = _dot(pooled, wpool_ref[...]) * pscale_ref[...]
    gp_a = gate_pre(0)
    y_d = _dot(mixed_d.astype(BF16), wbr_d_ref[...])
    merged = merged + gated(gp_d, y_d)

    y_a = _dot(o_ref[...], wbr_a_ref[...])
    half = t // 2
    for r in range(2):
        rows = slice(r * half, (r + 1) * half)
        m_r = merged[rows] + gated(gp_a[rows], y_a[rows])
        mo = _dot(m_r.astype(BF16), wout_ref[...])
        out_ref[rows, :] = x[rows] + _rms(mo, gpost_ref[...])


def _mixer(l, x2, o, w, seq):
    n = x2.shape[0]
    t = T_MIX
    return pl.pallas_call(
        functools.partial(_mixer_kernel, seq // t),
        grid=(n // t,),
        in_specs=[_row_spec(t, D_MODEL),
                  _row_spec(t, HEADS * VDIM),
                  _layer_spec(l, 1, D_MODEL),
                  _layer_spec(l, D_MODEL, REST_COLS),
                  _layer_spec(l, 1, N_BRANCH * D_MODEL),
                  _layer_spec(l, 1, SG_WIDTH),
                  _layer_spec(l, 1, SG_WIDTH),
                  _layer_spec(l, SG_CHUNK, SG_GROUPS * SG_CHUNK),
                  _layer_spec(l, SG_CHUNK, SG_WIDTH),
                  _layer_spec(l, 3, BR_WIDTH),
                  _layer_spec(l, BR_WIDTH, BR_WIDTH),
                  _layer_spec(l, 1, BR_WIDTH),
                  _layer_spec(l, BR_WIDTH, D_MODEL),
                  _layer_spec(l, BR_WIDTH, D_MODEL),
                  _layer_spec(l, BR_WIDTH, D_MODEL),
                  _layer_spec(l, BR_WIDTH, D_MODEL),
                  _layer_spec(l, D_MODEL, D_MODEL),
                  _layer_spec(l, 1, D_MODEL)],
        out_specs=_row_spec(t, D_MODEL),
        out_shape=jax.ShapeDtypeStruct((n, D_MODEL), F32),
        scratch_shapes=[pltpu.VMEM((t + HALO, BR_WIDTH), F32),
                        pltpu.VMEM((t + HALO, BR_WIDTH), F32)],
        compiler_params=_params(),
        name="mixer",
    )(x2, o, w["norm_mix_pre"], w["w_rest_in"], w["gate_b"], w["sg_ln_g"], w["sg_ln_b"], w["w_sg"],
      w["sg_bias"], w["conv_w"], w["w_pool"], w["pool_scale"], w["w_br_mla"], w["w_br_sg"],
      w["w_br_conv"], w["w_br_pool"], w["w_out"], w["norm_mix_post"])


def _ffn_kernel(x_ref, g_ref, w1_ref, w2_ref, gpost_ref, *rest):
    fused = len(rest) > 1
    out_ref = rest[-4] if fused else rest[0]
    x = x_ref[...]
    h = _rms(x, g_ref[...]).astype(BF16)
    a = jnp.maximum(_dot(h, w1_ref[...]), 0.0)
    a2 = (a * a).astype(BF16)
    half = x.shape[0] // 2
    halves = (slice(0, half), slice(half, 2 * half))
    f_halves = [_dot(a2[rows], w2_ref[...]) for rows in halves]
    for rows, f in zip(halves, f_halves):
        x_new = x[rows] + _rms(f, gpost_ref[...])
        out_ref[rows, :] = x_new
        if fused:
            _attn_proj_body(x_new, rows, *rest[:-4], *rest[-3:])


def _ffn(l, x2, w, tables=None):
    n = x2.shape[0]
    in_specs = [_row_spec(T_FFN, D_MODEL),
                _layer_spec(l, 1, D_MODEL),
                _layer_spec(l, D_MODEL, D_FF),
                _layer_spec(l, D_FF, D_MODEL),
                _layer_spec(l, 1, D_MODEL)]
    operands = [x2, w["norm_ffn_pre"], w["w_ff1"], w["w_ff2"], w["norm_ffn_post"]]
    out_specs = [_row_spec(T_FFN, D_MODEL)]
    out_shape = [jax.ShapeDtypeStruct((n, D_MODEL), F32)]
    if tables is not None:
        p_in, p_ops, p_out, p_shape = _attn_proj_io(l + 1, n, tables, w)
        in_specs += p_in
        operands += p_ops
        out_specs += p_out
        out_shape += p_shape
    return pl.pallas_call(
        _ffn_kernel,
        grid=(n // T_FFN,),
        in_specs=in_specs,
        out_specs=out_specs,
        out_shape=out_shape,
        compiler_params=_params(),
        name="ffn" if tables is None else "ffn_proj",
    )(*operands)


def _gather_cols(w, idx, sign):
    return jnp.take(w, jnp.asarray(idx), axis=-1) * jnp.asarray(sign, F32)


def _w_attn_layout_kernel(wt_ref, wa_ref):
    w = wt_ref[...].T
    r0 = Q_RANK + KV_RANK
    blk = w[:, r0:r0 + HEAD_LANES]
    lane = lax.broadcasted_iota(jnp.int32, blk.shape, 1)
    half = ROPE // 2
    lo = NOPE + ROPE
    rope = jnp.where((lane >= NOPE) & (lane < lo), pltpu.roll(blk, NOPE, 1),
                     jnp.where((lane >= lo) & (lane < lo + half), -pltpu.roll(blk, lo - half, 1),
                               jnp.where(lane >= lo + half, pltpu.roll(blk, lo + half, 1), 0.0)))
    wa_ref[...] = jnp.concatenate([w[:, :r0], rope], axis=1).astype(BF16)


def _w_rest_layout_kernel(wt_ref, wb_ref):
    scale = jnp.where(pl.program_id(1) >= GATE_COLS0 // W_COL_BLOCK, 0.5, 1.0)
    wb_ref[...] = (wt_ref[0] * scale).T.astype(BF16)


def _w_in_layout(w_in):
    w_t = jnp.swapaxes(w_in, 1, 2)
    params = pltpu.CompilerParams(dimension_semantics=("arbitrary", "arbitrary"), vmem_limit_bytes=VMEM_LIMIT)
    w_attn = pl.pallas_call(
        _w_attn_layout_kernel,
        grid=(DEPTH, 1),
        in_specs=[pl.BlockSpec((None, 4 * HEAD_LANES, D_MODEL), lambda l, c: (l, 0, 0))],
        out_specs=pl.BlockSpec((None, D_MODEL, ATTN_IN_COLS), lambda l, c: (l, 0, 0)),
        out_shape=jax.ShapeDtypeStruct((DEPTH, D_MODEL, ATTN_IN_COLS), BF16),
        compiler_params=params,
        name="w_attn_layout",
    )(w_t)
    w_rest = pl.pallas_call(
        _w_rest_layout_kernel,
        grid=(DEPTH, REST_COLS // W_COL_BLOCK),
        in_specs=[pl.BlockSpec((pl.Element(1), pl.Element(W_COL_BLOCK), pl.Element(D_MODEL)),
                               lambda l, c: (l, pl.multiple_of(ATTN_COLS + c * W_COL_BLOCK, 32), 0))],
        out_specs=pl.BlockSpec((None, D_MODEL, W_COL_BLOCK), lambda l, c: (l, 0, c)),
        out_shape=jax.ShapeDtypeStruct((DEPTH, D_MODEL, REST_COLS), BF16),
        compiler_params=params,
        name="w_rest_layout",
    )(w_t)
    return w_attn, w_rest


def _q_layout():
    nq = HEADS * HEAD_LANES
    idx = np.zeros(nq + HEADS * ROPE, np.int32)
    sign = np.zeros(nq + HEADS * ROPE, np.float32)
    half = ROPE // 2
    for h in range(HEADS):
        src = h * (NOPE + ROPE)
        dst = h * HEAD_LANES
        for j in range(NOPE + ROPE):
            idx[dst + j] = src + j
            sign[dst + j] = 1.0
        rot = nq + h * ROPE
        for j in range(half):
            idx[rot + j] = src + NOPE + half + j
            sign[rot + j] = -1.0
            idx[rot + half + j] = src + NOPE + j
            sign[rot + half + j] = 1.0
    return idx, sign


def _k_layout():
    nq = HEADS * HEAD_LANES
    idx = np.zeros(nq, np.int32)
    sign = np.zeros(nq, np.float32)
    for h in range(HEADS):
        src = h * (NOPE + VDIM)
        for j in range(NOPE):
            idx[h * HEAD_LANES + j] = src + j
            sign[h * HEAD_LANES + j] = 1.0
    return idx, sign


def _v_layout():
    idx = np.zeros(HEADS * VDIM, np.int32)
    for h in range(HEADS):
        for j in range(VDIM):
            idx[h * VDIM + j] = h * (NOPE + VDIM) + NOPE + j
    return idx, np.ones(HEADS * VDIM, np.float32)


def _prepare_weights(p):
    L = DEPTH
    row = lambda a: a.reshape(L, 1, a.shape[-1])
    eye = jnp.eye(len(POOL_WINDOWS), dtype=F32)
    w_attn_in, w_rest_in = _w_in_layout(p["w_in"])
    return {
        "norm_mix_pre": row(p["norm_mix_pre"]),
        "w_attn_in": w_attn_in,
        "q_norm": row(p["q_norm"]),
        "w_q_t": jnp.swapaxes(_gather_cols(p["w_uq"], *_q_layout()), 1, 2).astype(BF16),
        "kv_norm": row(p["kv_norm"]),
        "w_k": _gather_cols(p["w_ukv"], *_k_layout()).astype(BF16),
        "w_v_t": jnp.swapaxes(_gather_cols(p["w_ukv"], *_v_layout()), 1, 2).astype(BF16),
        "w_rest_in": w_rest_in,
        "gate_b": row(p["gate_b"]) * 0.5,
        "sg_ln_g": row(p["sg_ln_g"]),
        "sg_ln_b": row(p["sg_ln_b"]),
        "w_sg": jnp.transpose(p["sg_w"], (0, 2, 1, 3)).reshape(L, SG_CHUNK, SG_GROUPS * SG_CHUNK).astype(BF16),
        "sg_bias": jnp.repeat(jnp.transpose(p["sg_b"], (0, 2, 1)), SG_WIDTH // SG_GROUPS, axis=-1),
        "conv_w": p["conv_w"],
        "w_pool": jnp.einsum("lgcd,gh->lgchd", p["pool_w"], eye).reshape(L, BR_WIDTH, BR_WIDTH).astype(BF16),
        "pool_scale": row(p["pool_scale"]),
        "w_br_mla": (p["w_br_mla"] * 0.5).astype(BF16),
        "w_br_sg": (p["w_br_sg"] * 0.5).astype(BF16),
        "w_br_conv": (p["w_br_conv"] * 0.5).astype(BF16),
        "w_br_pool": (p["w_br_pool"] * 0.5).astype(BF16),
        "w_out": p["w_out"].astype(BF16),
        "norm_mix_post": row(p["norm_mix_post"]),
        "norm_ffn_pre": row(p["norm_ffn_pre"]),
        "w_ff1": p["w_ff1"].astype(BF16),
        "w_ff2": p["w_ff2"].astype(BF16),
        "norm_ffn_post": row(p["norm_ffn_post"]),
    }


def kernel(x, positions, norm_mix_pre, w_in, gate_b, q_norm, w_uq, kv_norm, w_ukv, w_br_mla, sg_ln_g, sg_ln_b, sg_w, sg_b, w_br_sg, conv_w, w_br_conv, pool_w, pool_scale, w_br_pool, w_out, norm_mix_post, norm_ffn_pre, w_ff1, w_ff2, norm_ffn_post):
    batch, seq, d = x.shape
    w = _prepare_weights(dict(
        norm_mix_pre=norm_mix_pre, w_in=w_in, gate_b=gate_b, q_norm=q_norm, w_uq=w_uq, kv_norm=kv_norm,
        w_ukv=w_ukv, w_br_mla=w_br_mla, sg_ln_g=sg_ln_g, sg_ln_b=sg_ln_b, sg_w=sg_w, sg_b=sg_b,
        w_br_sg=w_br_sg, conv_w=conv_w, w_br_conv=w_br_conv, pool_w=pool_w, pool_scale=pool_scale,
        w_br_pool=w_br_pool, w_out=w_out, norm_mix_post=norm_mix_post, norm_ffn_pre=norm_ffn_pre,
        w_ff1=w_ff1, w_ff2=w_ff2, norm_ffn_post=norm_ffn_post))
    tables = _rope_tables(positions)
    x2 = x.reshape(batch * seq, d)
    assert T_FFN == T_ATT
    q_t, k, v_t = _attn_proj(0, x2, tables, w)
    for l in range(DEPTH):
        o = _attention(q_t, k, v_t, batch, seq)
        x2 = _mixer(l, x2, o, w, seq)
        if l + 1 < DEPTH:
            x2, q_t, k, v_t = _ffn(l, x2, w, tables)
        else:
            (x2,) = _ffn(l, x2, w)
    return x2.reshape(batch, seq, d)
```

```python
import functools

import numpy as np
import jax
import jax.numpy as jnp
from jax import lax
from jax.experimental import pallas as pl
from jax.experimental.pallas import tpu as pltpu

D_MODEL = 1024
DEPTH = 4
HEADS = 4
NOPE = 64
ROPE = 32
VDIM = 64
Q_RANK = 256
KV_RANK = 128
ROPE_BASE = 10000.0
SG_WIDTH = 256
SG_GROUPS = 4
SG_CHUNK = 128
BR_WIDTH = 256
POOL_WINDOWS = (2, 4, 8, 16)
N_BRANCH = 4
D_FF = 4 * D_MODEL
EPS = 1e-6
HALO = 16
HEAD_LANES = 128
ATTN_COLS = Q_RANK + KV_RANK + ROPE
ATTN_IN_COLS = Q_RANK + KV_RANK + HEAD_LANES
GATE_COLS0 = 2 * SG_WIDTH + 3 * BR_WIDTH + BR_WIDTH
REST_COLS = GATE_COLS0 + N_BRANCH * D_MODEL
W_COL_BLOCK = 512
assert GATE_COLS0 % W_COL_BLOCK == 0 and REST_COLS % W_COL_BLOCK == 0

V_ROWS = VDIM + 16
Q_SCALE = (NOPE + ROPE) ** -0.5 * 1.4426950408889634

T_ROPE = 1024
T_ATT = 512
UNROLL = 4
assert UNROLL % 2 == 0
T_MIX = 512
T_FFN = 512

VMEM_LIMIT = 56 * 1024 * 1024

BF16 = jnp.bfloat16
F32 = jnp.float32


def _dot(a, b):
    return jnp.dot(a, b, preferred_element_type=F32)


def _dot_nt(a, b):
    return lax.dot_general(a, b, (((1,), (1,)), ((), ())), preferred_element_type=F32)


def _rms(x, g):
    return x * lax.rsqrt(jnp.mean(x * x, axis=-1, keepdims=True) + EPS) * g


def _params():
    return pltpu.CompilerParams(dimension_semantics=("arbitrary",), vmem_limit_bytes=VMEM_LIMIT)


def _layer_spec(l, *tail):
    zeros = (0,) * len(tail)
    return pl.BlockSpec((None,) + tuple(tail), lambda *_: (l,) + zeros,
                        pipeline_mode=pl.Buffered(1))


def _row_spec(t, n):
    return pl.BlockSpec((t, n), lambda i: (i, 0))


def _rope_table_kernel(pos_ref, freq_ref, cos_ref, sin_ref, cos_t_ref, sin_t_ref):
    t = pos_ref.shape[1]
    ang = freq_ref[...] * pos_ref[...].astype(F32)
    ones = lambda n: jnp.ones((n, t), F32)
    zeros = lambda n: jnp.zeros((n, t), F32)
    pad = HEAD_LANES - NOPE - ROPE
    cos_t = jnp.concatenate([ones(NOPE), jnp.cos(ang), ones(pad)], axis=0)
    sin_t = jnp.concatenate([zeros(NOPE), jnp.sin(ang), zeros(pad)], axis=0)
    cos_t_ref[...] = cos_t
    sin_t_ref[...] = sin_t
    cos_ref[...] = cos_t.T
    sin_ref[...] = sin_t.T


def _rope_tables(positions):
    n = positions.size
    inv_freq = ROPE_BASE ** (-jnp.arange(0, ROPE, 2, dtype=F32) / ROPE)
    freq = jnp.concatenate([inv_freq, inv_freq]).reshape(ROPE, 1)
    return pl.pallas_call(
        _rope_table_kernel,
        grid=(n // T_ROPE,),
        in_specs=[pl.BlockSpec((1, T_ROPE), lambda i: (0, i)),
                  pl.BlockSpec((ROPE, 1), lambda i: (0, 0))],
        out_specs=[_row_spec(T_ROPE, HEAD_LANES), _row_spec(T_ROPE, HEAD_LANES),
                   pl.BlockSpec((HEAD_LANES, T_ROPE), lambda i: (0, i)),
                   pl.BlockSpec((HEAD_LANES, T_ROPE), lambda i: (0, i))],
        out_shape=[jax.ShapeDtypeStruct((n, HEAD_LANES), F32)] * 2
        + [jax.ShapeDtypeStruct((HEAD_LANES, n), F32)] * 2,
        compiler_params=_params(),
        name="rope_tables",
    )(positions.reshape(1, n), freq)


def _attn_proj_body(x, rows, g_ref, wa_ref, qn_ref, wqt_ref, kvn_ref, wk_ref, wvt_ref,
                    cos_ref, sin_ref, cos_t_ref, sin_t_ref, qt_ref, k_ref, vt_ref):
    t = x.shape[0]
    h = _rms(x, g_ref[...]).astype(BF16)
    pa = _dot(h, wa_ref[...])
    c_q = pa[:, :Q_RANK]
    c_kv = pa[:, Q_RANK:Q_RANK + KV_RANK]
    kr = pa[:, Q_RANK + KV_RANK:]
    nq = HEADS * HEAD_LANES

    qlr_t = _dot_nt(wqt_ref[...], _rms(c_q, qn_ref[...]).astype(BF16))
    cos_t = cos_t_ref[:, rows]
    sin_rope_t = sin_t_ref[NOPE:NOPE + ROPE, rows]
    zeros = lambda n: jnp.zeros((n, t), F32)
    q_heads = []
    for hd in range(HEADS):
        rot = qlr_t[nq + hd * ROPE:nq + (hd + 1) * ROPE] * sin_rope_t
        q_heads.append(qlr_t[hd * HEAD_LANES:(hd + 1) * HEAD_LANES] * cos_t
                       + jnp.concatenate([zeros(NOPE), rot, zeros(HEAD_LANES - NOPE - ROPE)], axis=0))
    qt_ref[:, rows] = (jnp.concatenate(q_heads, axis=0) * Q_SCALE).astype(BF16)

    kvn = _rms(c_kv, kvn_ref[...]).astype(BF16)
    lane = lax.broadcasted_iota(jnp.int32, kr.shape, 1)
    k_rope = jnp.where(lane < NOPE + ROPE,
                       kr * cos_ref[rows, :] + pltpu.roll(kr, HEAD_LANES - ROPE, 1) * sin_ref[rows, :],
                       0.0)
    k = _dot(kvn, wk_ref[...]) + jnp.concatenate([k_rope] * HEADS, axis=1)
    k_ref[rows, :] = k.astype(BF16)

    v_t = _dot_nt(wvt_ref[...], kvn)
    ones_rows = (lax.broadcasted_iota(jnp.int32, (V_ROWS - VDIM, t), 0) == 0).astype(F32)
    pieces = []
    for hd in range(HEADS):
        pieces += [v_t[hd * VDIM:(hd + 1) * VDIM], ones_rows]
    vt_ref[:, rows] = jnp.concatenate(pieces, axis=0).astype(BF16)


def _attn_proj_kernel(x_ref, *refs):
    _attn_proj_body(x_ref[...], slice(None), *refs)


def _attn_proj_io(l, n, tables, w):
    nq = HEADS * HEAD_LANES
    col_spec = pl.BlockSpec((HEAD_LANES, T_ATT), lambda i: (0, i))
    in_specs = [_layer_spec(l, 1, D_MODEL),
                _layer_spec(l, D_MODEL, ATTN_IN_COLS),
                _layer_spec(l, 1, Q_RANK),
                _layer_spec(l, nq + HEADS * ROPE, Q_RANK),
                _layer_spec(l, 1, KV_RANK),
                _layer_spec(l, KV_RANK, nq),
                _layer_spec(l, HEADS * VDIM, KV_RANK),
                _row_spec(T_ATT, HEAD_LANES),
                _row_spec(T_ATT, HEAD_LANES),
                col_spec, col_spec]
    operands = [w["norm_mix_pre"], w["w_attn_in"], w["q_norm"], w["w_q_t"], w["kv_norm"], w["w_k"],
                w["w_v_t"], *tables]
    out_specs = [pl.BlockSpec((None, nq, T_ATT), lambda i: (i, 0, 0)),
                 _row_spec(T_ATT, nq),
                 pl.BlockSpec((None, HEADS * V_ROWS, T_ATT), lambda i: (i, 0, 0))]
    out_shape = [jax.ShapeDtypeStruct((n // T_ATT, nq, T_ATT), BF16),
                 jax.ShapeDtypeStruct((n, nq), BF16),
                 jax.ShapeDtypeStruct((n // T_ATT, HEADS * V_ROWS, T_ATT), BF16)]
    return in_specs, operands, out_specs, out_shape


def _attn_proj(l, x2, tables, w):
    n = x2.shape[0]
    in_specs, operands, out_specs, out_shape = _attn_proj_io(l, n, tables, w)
    return pl.pallas_call(
        _attn_proj_kernel,
        grid=(n // T_ATT,),
        in_specs=[_row_spec(T_ATT, D_MODEL)] + in_specs,
        out_specs=out_specs,
        out_shape=out_shape,
        compiler_params=_params(),
        name="attn_proj",
    )(x2, *operands)


def _attn_kernel(qt_ref, qn_ref, k_ref, kn_ref, vt_ref, o_ref, m_sc, acc_sc, sa_sc, sb_sc, ma_sc, mb_sc):
    b = pl.program_id(0)
    i = pl.program_id(1)
    m_sc[...] = jnp.full(m_sc.shape, -1e30, F32)
    acc_sc[...] = jnp.zeros(acc_sc.shape, F32)

    def produce_into(buf, h, k, q_src):
        s_ref, bm_ref = buf
        s_t = _dot(k, q_src[h * HEAD_LANES:(h + 1) * HEAD_LANES, :])
        s_ref[h] = s_t
        bm_ref[h] = jnp.max(s_t, axis=0, keepdims=True)

    def produce(j, h, buf):
        off = pl.multiple_of(j * T_ATT, T_ATT)
        produce_into(buf, h, k_ref[pl.ds(off, T_ATT), h * HEAD_LANES:(h + 1) * HEAD_LANES], qt_ref)

    def produce_next_tile(h, buf):
        produce_into(buf, h, kn_ref[:, h * HEAD_LANES:(h + 1) * HEAD_LANES], qn_ref)

    def accumulate(j, h, buf, masked):
        s_ref, bm_ref = buf
        s_t = s_ref[h]
        if masked:
            key = lax.broadcasted_iota(jnp.int32, s_t.shape, 0)
            qry = lax.broadcasted_iota(jnp.int32, s_t.shape, 1)
            s_t = jnp.where(key <= qry, s_t, -jnp.inf)
            blk_max = jnp.max(s_t, axis=0, keepdims=True)
        else:
            blk_max = bm_ref[h]
        m_prev = m_sc[h]
        m_new = jnp.maximum(m_prev, blk_max)
        alpha = jnp.exp2(m_prev - m_new)
        p_t = jnp.exp2(s_t - m_new).astype(BF16)
        v_t = vt_ref[j, h * V_ROWS:(h + 1) * V_ROWS, :]
        acc_sc[h] = alpha * acc_sc[h] + _dot(v_t, p_t)
        m_sc[h] = m_new

    def step(j, cur, nxt, diagonal):
        for h in range(HEADS):
            if diagonal:
                produce_next_tile(h, nxt)
            else:
                produce(j + 1, h, nxt)
            accumulate(j, h, cur, diagonal)

    def tile(first_buf, other_buf):
        bufs = (first_buf, other_buf)

        def chain(first, count, last_is_diagonal):
            for n in range(count):
                step(first + n, bufs[n % 2], bufs[(n + 1) % 2], last_is_diagonal and n == count - 1)

        def trip(jj, carry):
            chain(UNROLL * jj, UNROLL, False)
            return carry

        lax.fori_loop(0, i // UNROLL, trip, 0)
        base = (i // UNROLL) * UNROLL
        for rem in range(UNROLL):
            @pl.when(i - base == rem)
            def _(rem=rem):
                chain(base, rem + 1, True)

    buf_a = (sa_sc, ma_sc)
    buf_b = (sb_sc, mb_sc)

    @pl.when((b == 0) & (i == 0))
    def _():
        for h in range(HEADS):
            produce(0, h, buf_a)

    starts_in_b = ((i + 1) // 2) % 2

    @pl.when(starts_in_b == 0)
    def _():
        tile(buf_a, buf_b)

    @pl.when(starts_in_b == 1)
    def _():
        tile(buf_b, buf_a)

    outs = []
    for h in range(HEADS):
        acc = acc_sc[h]
        outs.append(acc[:VDIM] / acc[VDIM:VDIM + 1])
    o_ref[...] = jnp.concatenate(outs, axis=0).T.astype(BF16)


def _attention(q_t, k, v_t, batch, seq):
    nq = seq // T_ATT
    cols = HEADS * HEAD_LANES
    last = batch * nq - 1
    assert nq % 4 == 0
    nxt = lambda b, i: jnp.minimum(b * nq + i + 1, last)
    return pl.pallas_call(
        _attn_kernel,
        grid=(batch, nq),
        in_specs=[pl.BlockSpec((None, cols, T_ATT), lambda b, i: (b * nq + i, 0, 0)),
                  pl.BlockSpec((None, cols, T_ATT), lambda b, i: (nxt(b, i), 0, 0)),
                  pl.BlockSpec((seq, cols), lambda b, i: (b, 0)),
                  pl.BlockSpec((T_ATT, cols), lambda b, i: (nxt(b, i) // nq * nq, 0)),
                  pl.BlockSpec((nq, HEADS * V_ROWS, T_ATT), lambda b, i: (b, 0, 0))],
        out_specs=pl.BlockSpec((T_ATT, HEADS * VDIM), lambda b, i: (b * nq + i, 0)),
        out_shape=jax.ShapeDtypeStruct((batch * seq, HEADS * VDIM), BF16),
        scratch_shapes=[pltpu.VMEM((HEADS, 1, T_ATT), F32),
                        pltpu.VMEM((HEADS, V_ROWS, T_ATT), F32),
                        pltpu.VMEM((HEADS, T_ATT, T_ATT), F32),
                        pltpu.VMEM((HEADS, T_ATT, T_ATT), F32),
                        pltpu.VMEM((HEADS, 1, T_ATT), F32),
                        pltpu.VMEM((HEADS, 1, T_ATT), F32)],
        compiler_params=pltpu.CompilerParams(dimension_semantics=("arbitrary", "arbitrary"),
                                             vmem_limit_bytes=VMEM_LIMIT),
        name="attention",
    )(q_t, q_t, k, k, v_t)


def _mixer_kernel(tiles_per_seq, x_ref, o_ref, g_ref, wb_ref, gb_ref, lng_ref, lnb_ref, wsg_ref, sgb_ref,
                  cw_ref, wpool_ref, pscale_ref, wbr_a_ref, wbr_b_ref, wbr_c_ref, wbr_d_ref, wout_ref, gpost_ref,
                  out_ref, ez_ref, ep_ref):
    t = x_ref.shape[0]
    i = pl.program_id(0)
    seq_tile = lax.rem(i, tiles_per_seq)

    @pl.when(seq_tile == 0)
    def _():
        ez_ref[0:HALO, :] = jnp.zeros((HALO, BR_WIDTH), F32)
        ep_ref[0:HALO, :] = jnp.zeros((HALO, BR_WIDTH), F32)

    x = x_ref[...]
    h = _rms(x, g_ref[...]).astype(BF16)
    o0 = 2 * SG_WIDTH
    o1 = o0 + 3 * BR_WIDTH
    o2 = o1 + BR_WIDTH

    def gate_pre(br):
        return (_dot(h, wb_ref[:, o2 + br * D_MODEL:o2 + (br + 1) * D_MODEL])
                + gb_ref[:, br * D_MODEL:(br + 1) * D_MODEL])

    def gated(gp, y):
        return (1.0 + jnp.tanh(gp)) * y

    uv = _dot(h, wb_ref[:, 0:2 * SG_WIDTH])
    gp_b = gate_pre(1)
    u = jax.nn.gelu(uv[:, :SG_WIDTH])
    gv = jax.nn.gelu(uv[:, SG_WIDTH:])
    mu = jnp.mean(gv, axis=-1, keepdims=True)
    gc = gv - mu
    vln = gc * lax.rsqrt(jnp.mean(gc * gc, axis=-1, keepdims=True) + EPS) * lng_ref[...] + lnb_ref[...]
    vb = vln.astype(BF16)
    trow = lax.broadcasted_iota(jnp.int32, (SG_CHUNK, SG_GROUPS * SG_CHUNK), 0)
    scol = lax.rem(lax.broadcasted_iota(jnp.int32, (SG_CHUNK, SG_GROUPS * SG_CHUNK), 1), SG_CHUNK)
    wsg = jnp.where(scol <= trow, wsg_ref[...], jnp.zeros((), BF16))
    lane_group = lax.broadcasted_iota(jnp.int32, (SG_CHUNK, SG_WIDTH), 1) // (SG_WIDTH // SG_GROUPS)
    mixed_chunks = []
    for c in range(t // SG_CHUNK):
        vc = vb[c * SG_CHUNK:(c + 1) * SG_CHUNK]
        rhs = jnp.concatenate([jnp.where(lane_group == g, vc, jnp.zeros((), BF16))
                               for g in range(SG_GROUPS)], axis=0)
        mixed_chunks.append(_dot(wsg, rhs) + sgb_ref[...])

    cv = _dot(h, wb_ref[:, o0:o0 + 3 * BR_WIDTH])
    gp_c = gate_pre(2)
    y_b = _dot((u * jnp.concatenate(mixed_chunks, axis=0)).astype(BF16), wbr_b_ref[...])
    merged = gated(gp_b, y_b)
    z = cv[:, 2 * BR_WIDTH:] * cv[:, :BR_WIDTH]
    ez_ref[HALO:HALO + t, :] = z
    conv = (cw_ref[0:1, :] * ez_ref[HALO - 2:HALO - 2 + t, :]
            + cw_ref[1:2, :] * ez_ref[HALO - 1:HALO - 1 + t, :]
            + cw_ref[2:3, :] * z)
    ez_ref[0:HALO, :] = ez_ref[t:t + HALO, :]

    ep_ref[HALO:HALO + t, :] = _dot(h, wb_ref[:, o1:o1 + BR_WIDTH])
    gp_d = gate_pre(3)
    y_c = _dot((cv[:, BR_WIDTH:2 * BR_WIDTH] * conv).astype(BF16), wbr_c_ref[...])
    merged = merged + gated(gp_c, y_c)
    tpos = seq_tile * t + lax.broadcasted_iota(jnp.int32, (t, 128), 0) + 1
    lane_lo = lax.broadcasted_iota(jnp.int32, (t, 128), 1) < 64
    pooled = []
    for half, (w_lo, w_hi) in enumerate(((POOL_WINDOWS[0], POOL_WINDOWS[1]),
                                          (POOL_WINDOWS[2], POOL_WINDOWS[3]))):
        cols = slice(half * 128, (half + 1) * 128)
        cur = ep_ref[HALO:HALO + t, cols]
        run = cur
        for d in range(1, w_lo):
            run = run + ep_ref[HALO - d:HALO - d + t, cols]
        sum_lo = run
        for d in range(w_lo, w_hi):
            run = run + ep_ref[HALO - d:HALO - d + t, cols]
        win = jnp.where(lane_lo, w_lo, w_hi)
        cnt = jnp.minimum(tpos, win).astype(F32)
        pooled.append(jnp.where(lane_lo, sum_lo, run) / cnt - cur)
    pooled = jnp.concatenate(pooled, axis=1).astype(BF16)
    ep_ref[0:HALO, :] = ep_ref[t:t + HALO, :]
    mixed_d = _dot(pooled, wpool_ref[...]) * pscale_ref[...]
    gp_a = gate_pre(0)
    y_d = _dot(mixed_d.astype(BF16), wbr_d_ref[...])
    merged = merged + gated(gp_d, y_d)

    y_a = _dot(o_ref[...], wbr_a_ref[...])
    half = t // 2
    for r in range(2):
        rows = slice(r * half, (r + 1) * half)
        m_r = merged[rows] + gated(gp_a[rows], y_a[rows])
        mo = _dot(m_r.astype(BF16), wout_ref[...])
        out_ref[rows, :] = x[rows] + _rms(mo, gpost_ref[...])


def _mixer(l, x2, o, w, seq):
    n = x2.shape[0]
    t = T_MIX
    return pl.pallas_call(
        functools.partial(_mixer_kernel, seq // t),
        grid=(n // t,),
        in_specs=[_row_spec(t, D_MODEL),
                  _row_spec(t, HEADS * VDIM),
                  _layer_spec(l, 1, D_MODEL),
                  _layer_spec(l, D_MODEL, REST_COLS),
                  _layer_spec(l, 1, N_BRANCH * D_MODEL),
                  _layer_spec(l, 1, SG_WIDTH),
                  _layer_spec(l, 1, SG_WIDTH),
                  _layer_spec(l, SG_CHUNK, SG_GROUPS * SG_CHUNK),
                  _layer_spec(l, SG_CHUNK, SG_WIDTH),
                  _layer_spec(l, 3, BR_WIDTH),
                  _layer_spec(l, BR_WIDTH, BR_WIDTH),
                  _layer_spec(l, 1, BR_WIDTH),
                  _layer_spec(l, BR_WIDTH, D_MODEL),
                  _layer_spec(l, BR_WIDTH, D_MODEL),
                  _layer_spec(l, BR_WIDTH, D_MODEL),
                  _layer_spec(l, BR_WIDTH, D_MODEL),
                  _layer_spec(l, D_MODEL, D_MODEL),
                  _layer_spec(l, 1, D_MODEL)],
        out_specs=_row_spec(t, D_MODEL),
        out_shape=jax.ShapeDtypeStruct((n, D_MODEL), F32),
        scratch_shapes=[pltpu.VMEM((t + HALO, BR_WIDTH), F32),
                        pltpu.VMEM((t + HALO, BR_WIDTH), F32)],
        compiler_params=_params(),
        name="mixer",
    )(x2, o, w["norm_mix_pre"], w["w_rest_in"], w["gate_b"], w["sg_ln_g"], w["sg_ln_b"], w["w_sg"],
      w["sg_bias"], w["conv_w"], w["w_pool"], w["pool_scale"], w["w_br_mla"], w["w_br_sg"],
      w["w_br_conv"], w["w_br_pool"], w["w_out"], w["norm_mix_post"])


def _ffn_kernel(x_ref, g_ref, w1_ref, w2_ref, gpost_ref, *rest):
    fused = len(rest) > 1
    out_ref = rest[-4] if fused else rest[0]
    x = x_ref[...]
    h = _rms(x, g_ref[...]).astype(BF16)
    a = jnp.maximum(_dot(h, w1_ref[...]), 0.0)
    a2 = (a * a).astype(BF16)
    half = x.shape[0] // 2
    halves = (slice(0, half), slice(half, 2 * half))
    f_halves = [_dot(a2[rows], w2_ref[...]) for rows in halves]
    for rows, f in zip(halves, f_halves):
        x_new = x[rows] + _rms(f, gpost_ref[...])
        out_ref[rows, :] = x_new
        if fused:
            _attn_proj_body(x_new, rows, *rest[:-4], *rest[-3:])


def _ffn(l, x2, w, tables=None):
    n = x2.shape[0]
    in_specs = [_row_spec(T_FFN, D_MODEL),
                _layer_spec(l, 1, D_MODEL),
                _layer_spec(l, D_MODEL, D_FF),
                _layer_spec(l, D_FF, D_MODEL),
                _layer_spec(l, 1, D_MODEL)]
    operands = [x2, w["norm_ffn_pre"], w["w_ff1"], w["w_ff2"], w["norm_ffn_post"]]
    out_specs = [_row_spec(T_FFN, D_MODEL)]
    out_shape = [jax.ShapeDtypeStruct((n, D_MODEL), F32)]
    if tables is not None:
        p_in, p_ops, p_out, p_shape = _attn_proj_io(l + 1, n, tables, w)
        in_specs += p_in
        operands += p_ops
        out_specs += p_out
        out_shape += p_shape
    return pl.pallas_call(
        _ffn_kernel,
        grid=(n // T_FFN,),
        in_specs=in_specs,
        out_specs=out_specs,
        out_shape=out_shape,
        compiler_params=_params(),
        name="ffn" if tables is None else "ffn_proj",
    )(*operands)


def _gather_cols(w, idx, sign):
    return jnp.take(w, jnp.asarray(idx), axis=-1) * jnp.asarray(sign, F32)


def _w_attn_layout_kernel(wt_ref, wa_ref):
    w = wt_ref[...].T
    r0 = Q_RANK + KV_RANK
    blk = w[:, r0:r0 + HEAD_LANES]
    lane = lax.broadcasted_iota(jnp.int32, blk.shape, 1)
    half = ROPE // 2
    lo = NOPE + ROPE
    rope = jnp.where((lane >= NOPE) & (lane < lo), pltpu.roll(blk, NOPE, 1),
                     jnp.where((lane >= lo) & (lane < lo + half), -pltpu.roll(blk, lo - half, 1),
                               jnp.where(lane >= lo + half, pltpu.roll(blk, lo + half, 1), 0.0)))
    wa_ref[...] = jnp.concatenate([w[:, :r0], rope], axis=1).astype(BF16)


def _w_rest_layout_kernel(wt_ref, wb_ref):
    scale = jnp.where(pl.program_id(1) >= GATE_COLS0 // W_COL_BLOCK, 0.5, 1.0)
    wb_ref[...] = (wt_ref[0] * scale).T.astype(BF16)


def _w_in_layout(w_in):
    w_t = jnp.swapaxes(w_in, 1, 2)
    params = pltpu.CompilerParams(dimension_semantics=("arbitrary", "arbitrary"), vmem_limit_bytes=VMEM_LIMIT)
    w_attn = pl.pallas_call(
        _w_attn_layout_kernel,
        grid=(DEPTH, 1),
        in_specs=[pl.BlockSpec((None, 4 * HEAD_LANES, D_MODEL), lambda l, c: (l, 0, 0))],
        out_specs=pl.BlockSpec((None, D_MODEL, ATTN_IN_COLS), lambda l, c: (l, 0, 0)),
        out_shape=jax.ShapeDtypeStruct((DEPTH, D_MODEL, ATTN_IN_COLS), BF16),
        compiler_params=params,
        name="w_attn_layout",
    )(w_t)
    w_rest = pl.pallas_call(
        _w_rest_layout_kernel,
        grid=(DEPTH, REST_COLS // W_COL_BLOCK),
        in_specs=[pl.BlockSpec((pl.Element(1), pl.Element(W_COL_BLOCK), pl.Element(D_MODEL)),
                               lambda l, c: (l, pl.multiple_of(ATTN_COLS + c * W_COL_BLOCK, 32), 0))],
        out_specs=pl.BlockSpec((None, D_MODEL, W_COL_BLOCK), lambda l, c: (l, 0, c)),
        out_shape=jax.ShapeDtypeStruct((DEPTH, D_MODEL, REST_COLS), BF16),
        compiler_params=params,
        name="w_rest_layout",
    )(w_t)
    return w_attn, w_rest


def _q_layout():
    nq = HEADS * HEAD_LANES
    idx = np.zeros(nq + HEADS * ROPE, np.int32)
    sign = np.zeros(nq + HEADS * ROPE, np.float32)
    half = ROPE // 2
    for h in range(HEADS):
        src = h * (NOPE + ROPE)
        dst = h * HEAD_LANES
        for j in range(NOPE + ROPE):
            idx[dst + j] = src + j
            sign[dst + j] = 1.0
        rot = nq + h * ROPE
        for j in range(half):
            idx[rot + j] = src + NOPE + half + j
            sign[rot + j] = -1.0
            idx[rot + half + j] = src + NOPE + j
            sign[rot + half + j] = 1.0
    return idx, sign


def _k_layout():
    nq = HEADS * HEAD_LANES
    idx = np.zeros(nq, np.int32)
    sign = np.zeros(nq, np.float32)
    for h in range(HEADS):
        src = h * (NOPE + VDIM)
        for j in range(NOPE):
            idx[h * HEAD_LANES + j] = src + j
            sign[h * HEAD_LANES + j] = 1.0
    return idx, sign


def _v_layout():
    idx = np.zeros(HEADS * VDIM, np.int32)
    for h in range(HEADS):
        for j in range(VDIM):
            idx[h * VDIM + j] = h * (NOPE + VDIM) + NOPE + j
    return idx, np.ones(HEADS * VDIM, np.float32)


def _prepare_weights(p):
    L = DEPTH
    row = lambda a: a.reshape(L, 1, a.shape[-1])
    eye = jnp.eye(len(POOL_WINDOWS), dtype=F32)
    w_attn_in, w_rest_in = _w_in_layout(p["w_in"])
    return {
        "norm_mix_pre": row(p["norm_mix_pre"]),
        "w_attn_in": w_attn_in,
        "q_norm": row(p["q_norm"]),
        "w_q_t": jnp.swapaxes(_gather_cols(p["w_uq"], *_q_layout()), 1, 2).astype(BF16),
        "kv_norm": row(p["kv_norm"]),
        "w_k": _gather_cols(p["w_ukv"], *_k_layout()).astype(BF16),
        "w_v_t": jnp.swapaxes(_gather_cols(p["w_ukv"], *_v_layout()), 1, 2).astype(BF16),
        "w_rest_in": w_rest_in,
        "gate_b": row(p["gate_b"]) * 0.5,
        "sg_ln_g": row(p["sg_ln_g"]),
        "sg_ln_b": row(p["sg_ln_b"]),
        "w_sg": jnp.transpose(p["sg_w"], (0, 2, 1, 3)).reshape(L, SG_CHUNK, SG_GROUPS * SG_CHUNK).astype(BF16),
        "sg_bias": jnp.repeat(jnp.transpose(p["sg_b"], (0, 2, 1)), SG_WIDTH // SG_GROUPS, axis=-1),
        "conv_w": p["conv_w"],
        "w_pool": jnp.einsum("lgcd,gh->lgchd", p["pool_w"], eye).reshape(L, BR_WIDTH, BR_WIDTH).astype(BF16),
        "pool_scale": row(p["pool_scale"]),
        "w_br_mla": (p["w_br_mla"] * 0.5).astype(BF16),
        "w_br_sg": (p["w_br_sg"] * 0.5).astype(BF16),
        "w_br_conv": (p["w_br_conv"] * 0.5).astype(BF16),
        "w_br_pool": (p["w_br_pool"] * 0.5).astype(BF16),
        "w_out": p["w_out"].astype(BF16),
        "norm_mix_post": row(p["norm_mix_post"]),
        "norm_ffn_pre": row(p["norm_ffn_pre"]),
        "w_ff1": p["w_ff1"].astype(BF16),
        "w_ff2": p["w_ff2"].astype(BF16),
        "norm_ffn_post": row(p["norm_ffn_post"]),
    }


def kernel(x, positions, norm_mix_pre, w_in, gate_b, q_norm, w_uq, kv_norm, w_ukv, w_br_mla, sg_ln_g, sg_ln_b, sg_w, sg_b, w_br_sg, conv_w, w_br_conv, pool_w, pool_scale, w_br_pool, w_out, norm_mix_post, norm_ffn_pre, w_ff1, w_ff2, norm_ffn_post):
    batch, seq, d = x.shape
    w = _prepare_weights(dict(
        norm_mix_pre=norm_mix_pre, w_in=w_in, gate_b=gate_b, q_norm=q_norm, w_uq=w_uq, kv_norm=kv_norm,
        w_ukv=w_ukv, w_br_mla=w_br_mla, sg_ln_g=sg_ln_g, sg_ln_b=sg_ln_b, sg_w=sg_w, sg_b=sg_b,
        w_br_sg=w_br_sg, conv_w=conv_w, w_br_conv=w_br_conv, pool_w=pool_w, pool_scale=pool_scale,
        w_br_pool=w_br_pool, w_out=w_out, norm_mix_post=norm_mix_post, norm_ffn_pre=norm_ffn_pre,
        w_ff1=w_ff1, w_ff2=w_ff2, norm_ffn_post=norm_ffn_post))
    tables = _rope_tables(positions)
    x2 = x.reshape(batch * seq, d)
    assert T_FFN == T_ATT
    q_t, k, v_t = _attn_proj(0, x2, tables, w)
    for l in range(DEPTH):
        o = _attention(q_t, k, v_t, batch, seq)
        x2 = _mixer(l, x2, o, w, seq)
        if l + 1 < DEPTH:
            x2, q_t, k, v_t = _ffn(l, x2, w, tables)
        else:
            (x2,) = _ffn(l, x2, w)
    return x2.reshape(batch, seq, d)
```

```python
import functools

import numpy as np
import jax
import jax.numpy as jnp
from jax import lax
from jax.experimental import pallas as pl
from jax.experimental.pallas import tpu as pltpu

D_MODEL = 1024
DEPTH = 4
HEADS = 4
NOPE = 64
ROPE = 32
VDIM = 64
Q_RANK = 256
KV_RANK = 128
ROPE_BASE = 10000.0
SG_WIDTH = 256
SG_GROUPS = 4
SG_CHUNK = 128
BR_WIDTH = 256
POOL_WINDOWS = (2, 4, 8, 16)
POOL_GROUP = BR_WIDTH // len(POOL_WINDOWS)
LANES = 128
assert 2 * POOL_GROUP == LANES
N_BRANCH = 4
D_FF = 4 * D_MODEL
EPS = 1e-6
HALO = 16
HEAD_LANES = 128
ATTN_COLS = Q_RANK + KV_RANK + ROPE
ATTN_IN_COLS = Q_RANK + KV_RANK + HEAD_LANES
GATE_COLS0 = 2 * SG_WIDTH + 3 * BR_WIDTH + BR_WIDTH
REST_COLS = GATE_COLS0 + N_BRANCH * D_MODEL
W_COL_BLOCK = 512
assert GATE_COLS0 % W_COL_BLOCK == 0 and REST_COLS % W_COL_BLOCK == 0

BF16_SUBLANES = 16
V_ROWS = VDIM + BF16_SUBLANES
LOG2_E = 1.4426950408889634
Q_SCALE = (NOPE + ROPE) ** -0.5 * LOG2_E

T_ROPE = 1024
T_ATT = 512
UNROLL = 4
assert UNROLL % 2 == 0
T_MIX = 512
T_FFN = 512

V7X_VMEM_BYTES = 64 * 1024 * 1024
VMEM_LIMIT = V7X_VMEM_BYTES * 7 // 8

BF16 = jnp.bfloat16
F32 = jnp.float32


def _dot(a, b):
    return jnp.dot(a, b, preferred_element_type=F32)


def _dot_nt(a, b):
    return lax.dot_general(a, b, (((1,), (1,)), ((), ())), preferred_element_type=F32)


def _rms(x, g):
    return x * lax.rsqrt(jnp.mean(x * x, axis=-1, keepdims=True) + EPS) * g


def _params():
    return pltpu.CompilerParams(dimension_semantics=("arbitrary",), vmem_limit_bytes=VMEM_LIMIT)


def _layer_spec(l, *tail):
    zeros = (0,) * len(tail)
    return pl.BlockSpec((None,) + tuple(tail), lambda *_: (l,) + zeros,
                        pipeline_mode=pl.Buffered(1))


def _row_spec(t, n):
    return pl.BlockSpec((t, n), lambda i: (i, 0))


def _rope_table_kernel(pos_ref, freq_ref, cos_ref, sin_ref, cos_t_ref, sin_t_ref):
    t = pos_ref.shape[1]
    ang = freq_ref[...] * pos_ref[...].astype(F32)
    ones = lambda n: jnp.ones((n, t), F32)
    zeros = lambda n: jnp.zeros((n, t), F32)
    pad = HEAD_LANES - NOPE - ROPE
    cos_t = jnp.concatenate([ones(NOPE), jnp.cos(ang), ones(pad)], axis=0)
    sin_t = jnp.concatenate([zeros(NOPE), jnp.sin(ang), zeros(pad)], axis=0)
    cos_t_ref[...] = cos_t
    sin_t_ref[...] = sin_t
    cos_ref[...] = cos_t.T
    sin_ref[...] = sin_t.T


def _rope_tables(positions):
    n = positions.size
    inv_freq = ROPE_BASE ** (-jnp.arange(0, ROPE, 2, dtype=F32) / ROPE)
    freq = jnp.concatenate([inv_freq, inv_freq]).reshape(ROPE, 1)
    return pl.pallas_call(
        _rope_table_kernel,
        grid=(n // T_ROPE,),
        in_specs=[pl.BlockSpec((1, T_ROPE), lambda i: (0, i)),
                  pl.BlockSpec((ROPE, 1), lambda i: (0, 0))],
        out_specs=[_row_spec(T_ROPE, HEAD_LANES), _row_spec(T_ROPE, HEAD_LANES),
                   pl.BlockSpec((HEAD_LANES, T_ROPE), lambda i: (0, i)),
                   pl.BlockSpec((HEAD_LANES, T_ROPE), lambda i: (0, i))],
        out_shape=[jax.ShapeDtypeStruct((n, HEAD_LANES), F32)] * 2
        + [jax.ShapeDtypeStruct((HEAD_LANES, n), F32)] * 2,
        compiler_params=_params(),
        name="rope_tables",
    )(positions.reshape(1, n), freq)


def _attn_proj_body(x, rows, g_ref, wa_ref, qn_ref, wqt_ref, kvn_ref, wk_ref, wvt_ref,
                    cos_ref, sin_ref, cos_t_ref, sin_t_ref, qt_ref, k_ref, vt_ref):
    t = x.shape[0]
    h = _rms(x, g_ref[...]).astype(BF16)
    pa = _dot(h, wa_ref[...])
    c_q = pa[:, :Q_RANK]
    c_kv = pa[:, Q_RANK:Q_RANK + KV_RANK]
    kr = pa[:, Q_RANK + KV_RANK:]
    nq = HEADS * HEAD_LANES

    qlr_t = _dot_nt(wqt_ref[...], _rms(c_q, qn_ref[...]).astype(BF16))
    cos_t = cos_t_ref[:, rows]
    sin_rope_t = sin_t_ref[NOPE:NOPE + ROPE, rows]
    zeros = lambda n: jnp.zeros((n, t), F32)
    q_heads = []
    for hd in range(HEADS):
        rot = qlr_t[nq + hd * ROPE:nq + (hd + 1) * ROPE] * sin_rope_t
        q_heads.append(qlr_t[hd * HEAD_LANES:(hd + 1) * HEAD_LANES] * cos_t
                       + jnp.concatenate([zeros(NOPE), rot, zeros(HEAD_LANES - NOPE - ROPE)], axis=0))
    qt_ref[:, rows] = (jnp.concatenate(q_heads, axis=0) * Q_SCALE).astype(BF16)

    kvn = _rms(c_kv, kvn_ref[...]).astype(BF16)
    lane = lax.broadcasted_iota(jnp.int32, kr.shape, 1)
    k_rope = jnp.where(lane < NOPE + ROPE,
                       kr * cos_ref[rows, :] + pltpu.roll(kr, HEAD_LANES - ROPE, 1) * sin_ref[rows, :],
                       0.0)
    k = _dot(kvn, wk_ref[...]) + jnp.concatenate([k_rope] * HEADS, axis=1)
    k_ref[rows, :] = k.astype(BF16)

    v_t = _dot_nt(wvt_ref[...], kvn)
    ones_rows = (lax.broadcasted_iota(jnp.int32, (V_ROWS - VDIM, t), 0) == 0).astype(F32)
    pieces = []
    for hd in range(HEADS):
        pieces += [v_t[hd * VDIM:(hd + 1) * VDIM], ones_rows]
    vt_ref[:, rows] = jnp.concatenate(pieces, axis=0).astype(BF16)


def _attn_proj_kernel(x_ref, *refs):
    _attn_proj_body(x_ref[...], slice(None), *refs)


def _attn_proj_io(l, n, tables, w):
    nq = HEADS * HEAD_LANES
    col_spec = pl.BlockSpec((HEAD_LANES, T_ATT), lambda i: (0, i))
    in_specs = [_layer_spec(l, 1, D_MODEL),
                _layer_spec(l, D_MODEL, ATTN_IN_COLS),
                _layer_spec(l, 1, Q_RANK),
                _layer_spec(l, nq + HEADS * ROPE, Q_RANK),
                _layer_spec(l, 1, KV_RANK),
                _layer_spec(l, KV_RANK, nq),
                _layer_spec(l, HEADS * VDIM, KV_RANK),
                _row_spec(T_ATT, HEAD_LANES),
                _row_spec(T_ATT, HEAD_LANES),
                col_spec, col_spec]
    operands = [w["norm_mix_pre"], w["w_attn_in"], w["q_norm"], w["w_q_t"], w["kv_norm"], w["w_k"],
                w["w_v_t"], *tables]
    out_specs = [pl.BlockSpec((None, nq, T_ATT), lambda i: (i, 0, 0)),
                 _row_spec(T_ATT, nq),
                 pl.BlockSpec((None, HEADS * V_ROWS, T_ATT), lambda i: (i, 0, 0))]
    out_shape = [jax.ShapeDtypeStruct((n // T_ATT, nq, T_ATT), BF16),
                 jax.ShapeDtypeStruct((n, nq), BF16),
                 jax.ShapeDtypeStruct((n // T_ATT, HEADS * V_ROWS, T_ATT), BF16)]
    return in_specs, operands, out_specs, out_shape


def _attn_proj(l, x2, tables, w):
    n = x2.shape[0]
    in_specs, operands, out_specs, out_shape = _attn_proj_io(l, n, tables, w)
    return pl.pallas_call(
        _attn_proj_kernel,
        grid=(n // T_ATT,),
        in_specs=[_row_spec(T_ATT, D_MODEL)] + in_specs,
        out_specs=out_specs,
        out_shape=out_shape,
        compiler_params=_params(),
        name="attn_proj",
    )(x2, *operands)


def _attn_kernel(qt_ref, qn_ref, k_ref, kn_ref, vt_ref, o_ref, m_sc, acc_sc, sa_sc, sb_sc, ma_sc, mb_sc):
    b = pl.program_id(0)
    i = pl.program_id(1)
    m_sc[...] = jnp.full(m_sc.shape, -1e30, F32)
    acc_sc[...] = jnp.zeros(acc_sc.shape, F32)

    def produce_into(buf, h, k, q_src):
        s_ref, bm_ref = buf
        s_t = _dot(k, q_src[h * HEAD_LANES:(h + 1) * HEAD_LANES, :])
        s_ref[h] = s_t
        bm_ref[h] = jnp.max(s_t, axis=0, keepdims=True)

    def produce(j, h, buf):
        off = pl.multiple_of(j * T_ATT, T_ATT)
        produce_into(buf, h, k_ref[pl.ds(off, T_ATT), h * HEAD_LANES:(h + 1) * HEAD_LANES], qt_ref)

    def produce_next_tile(h, buf):
        produce_into(buf, h, kn_ref[:, h * HEAD_LANES:(h + 1) * HEAD_LANES], qn_ref)

    def accumulate(j, h, buf, masked):
        s_ref, bm_ref = buf
        s_t = s_ref[h]
        if masked:
            key = lax.broadcasted_iota(jnp.int32, s_t.shape, 0)
            qry = lax.broadcasted_iota(jnp.int32, s_t.shape, 1)
            s_t = jnp.where(key <= qry, s_t, -jnp.inf)
            blk_max = jnp.max(s_t, axis=0, keepdims=True)
        else:
            blk_max = bm_ref[h]
        m_prev = m_sc[h]
        m_new = jnp.maximum(m_prev, blk_max)
        alpha = jnp.exp2(m_prev - m_new)
        p_t = jnp.exp2(s_t - m_new).astype(BF16)
        v_t = vt_ref[j, h * V_ROWS:(h + 1) * V_ROWS, :]
        acc_sc[h] = alpha * acc_sc[h] + _dot(v_t, p_t)
        m_sc[h] = m_new

    def step(j, cur, nxt, diagonal):
        for h in range(HEADS):
            if diagonal:
                produce_next_tile(h, nxt)
            else:
                produce(j + 1, h, nxt)
            accumulate(j, h, cur, diagonal)

    def tile(first_buf, other_buf):
        bufs = (first_buf, other_buf)

        def chain(first, count, last_is_diagonal):
            for n in range(count):
                step(first + n, bufs[n % 2], bufs[(n + 1) % 2], last_is_diagonal and n == count - 1)

        def trip(jj, carry):
            chain(UNROLL * jj, UNROLL, False)
            return carry

        lax.fori_loop(0, i // UNROLL, trip, 0)
        base = (i // UNROLL) * UNROLL
        for rem in range(UNROLL):
            @pl.when(i - base == rem)
            def _(rem=rem):
                chain(base, rem + 1, True)

    buf_a = (sa_sc, ma_sc)
    buf_b = (sb_sc, mb_sc)

    @pl.when((b == 0) & (i == 0))
    def _():
        for h in range(HEADS):
            produce(0, h, buf_a)

    starts_in_b = ((i + 1) // 2) % 2

    @pl.when(starts_in_b == 0)
    def _():
        tile(buf_a, buf_b)

    @pl.when(starts_in_b == 1)
    def _():
        tile(buf_b, buf_a)

    outs = []
    for h in range(HEADS):
        acc = acc_sc[h]
        outs.append(acc[:VDIM] / acc[VDIM:VDIM + 1])
    o_ref[...] = jnp.concatenate(outs, axis=0).T.astype(BF16)


def _attention(q_t, k, v_t, batch, seq):
    nq = seq // T_ATT
    cols = HEADS * HEAD_LANES
    last = batch * nq - 1
    assert nq % 4 == 0
    nxt = lambda b, i: jnp.minimum(b * nq + i + 1, last)
    return pl.pallas_call(
        _attn_kernel,
        grid=(batch, nq),
        in_specs=[pl.BlockSpec((None, cols, T_ATT), lambda b, i: (b * nq + i, 0, 0)),
                  pl.BlockSpec((None, cols, T_ATT), lambda b, i: (nxt(b, i), 0, 0)),
                  pl.BlockSpec((seq, cols), lambda b, i: (b, 0)),
                  pl.BlockSpec((T_ATT, cols), lambda b, i: (nxt(b, i) // nq * nq, 0)),
                  pl.BlockSpec((nq, HEADS * V_ROWS, T_ATT), lambda b, i: (b, 0, 0))],
        out_specs=pl.BlockSpec((T_ATT, HEADS * VDIM), lambda b, i: (b * nq + i, 0)),
        out_shape=jax.ShapeDtypeStruct((batch * seq, HEADS * VDIM), BF16),
        scratch_shapes=[pltpu.VMEM((HEADS, 1, T_ATT), F32),
                        pltpu.VMEM((HEADS, V_ROWS, T_ATT), F32),
                        pltpu.VMEM((HEADS, T_ATT, T_ATT), F32),
                        pltpu.VMEM((HEADS, T_ATT, T_ATT), F32),
                        pltpu.VMEM((HEADS, 1, T_ATT), F32),
                        pltpu.VMEM((HEADS, 1, T_ATT), F32)],
        compiler_params=pltpu.CompilerParams(dimension_semantics=("arbitrary", "arbitrary"),
                                             vmem_limit_bytes=VMEM_LIMIT),
        name="attention",
    )(q_t, q_t, k, k, v_t)


def _mixer_kernel(tiles_per_seq, x_ref, o_ref, g_ref, wb_ref, gb_ref, lng_ref, lnb_ref, wsg_ref, sgb_ref,
                  cw_ref, wpool_ref, pscale_ref, wbr_a_ref, wbr_b_ref, wbr_c_ref, wbr_d_ref, wout_ref, gpost_ref,
                  out_ref, ez_ref, ep_ref):
    t = x_ref.shape[0]
    i = pl.program_id(0)
    seq_tile = lax.rem(i, tiles_per_seq)

    @pl.when(seq_tile == 0)
    def _():
        ez_ref[0:HALO, :] = jnp.zeros((HALO, BR_WIDTH), F32)
        ep_ref[0:HALO, :] = jnp.zeros((HALO, BR_WIDTH), F32)

    x = x_ref[...]
    h = _rms(x, g_ref[...]).astype(BF16)
    o0 = 2 * SG_WIDTH
    o1 = o0 + 3 * BR_WIDTH
    o2 = o1 + BR_WIDTH

    def gate_pre(br):
        return (_dot(h, wb_ref[:, o2 + br * D_MODEL:o2 + (br + 1) * D_MODEL])
                + gb_ref[:, br * D_MODEL:(br + 1) * D_MODEL])

    def gated(gp, y):
        return (1.0 + jnp.tanh(gp)) * y

    uv = _dot(h, wb_ref[:, 0:2 * SG_WIDTH])
    gp_b = gate_pre(1)
    u = jax.nn.gelu(uv[:, :SG_WIDTH])
    gv = jax.nn.gelu(uv[:, SG_WIDTH:])
    mu = jnp.mean(gv, axis=-1, keepdims=True)
    gc = gv - mu
    vln = gc * lax.rsqrt(jnp.mean(gc * gc, axis=-1, keepdims=True) + EPS) * lng_ref[...] + lnb_ref[...]
    vb = vln.astype(BF16)
    trow = lax.broadcasted_iota(jnp.int32, (SG_CHUNK, SG_GROUPS * SG_CHUNK), 0)
    scol = lax.rem(lax.broadcasted_iota(jnp.int32, (SG_CHUNK, SG_GROUPS * SG_CHUNK), 1), SG_CHUNK)
    wsg = jnp.where(scol <= trow, wsg_ref[...], jnp.zeros((), BF16))
    lane_group = lax.broadcasted_iota(jnp.int32, (SG_CHUNK, SG_WIDTH), 1) // (SG_WIDTH // SG_GROUPS)
    mixed_chunks = []
    for c in range(t // SG_CHUNK):
        vc = vb[c * SG_CHUNK:(c + 1) * SG_CHUNK]
        rhs = jnp.concatenate([jnp.where(lane_group == g, vc, jnp.zeros((), BF16))
                               for g in range(SG_GROUPS)], axis=0)
        mixed_chunks.append(_dot(wsg, rhs) + sgb_ref[...])

    cv = _dot(h, wb_ref[:, o0:o0 + 3 * BR_WIDTH])
    gp_c = gate_pre(2)
    y_b = _dot((u * jnp.concatenate(mixed_chunks, axis=0)).astype(BF16), wbr_b_ref[...])
    merged = gated(gp_b, y_b)
    z = cv[:, 2 * BR_WIDTH:] * cv[:, :BR_WIDTH]
    ez_ref[HALO:HALO + t, :] = z
    conv = (cw_ref[0:1, :] * ez_ref[HALO - 2:HALO - 2 + t, :]
            + cw_ref[1:2, :] * ez_ref[HALO - 1:HALO - 1 + t, :]
            + cw_ref[2:3, :] * z)
    ez_ref[0:HALO, :] = ez_ref[t:t + HALO, :]

    ep_ref[HALO:HALO + t, :] = _dot(h, wb_ref[:, o1:o1 + BR_WIDTH])
    gp_d = gate_pre(3)
    y_c = _dot((cv[:, BR_WIDTH:2 * BR_WIDTH] * conv).astype(BF16), wbr_c_ref[...])
    merged = merged + gated(gp_c, y_c)
    tpos = seq_tile * t + lax.broadcasted_iota(jnp.int32, (t, LANES), 0) + 1
    lane_lo = lax.broadcasted_iota(jnp.int32, (t, LANES), 1) < POOL_GROUP
    pooled = []
    for half, (w_lo, w_hi) in enumerate(((POOL_WINDOWS[0], POOL_WINDOWS[1]),
                                          (POOL_WINDOWS[2], POOL_WINDOWS[3]))):
        cols = slice(half * LANES, (half + 1) * LANES)
        cur = ep_ref[HALO:HALO + t, cols]
        run = cur
        for d in range(1, w_lo):
            run = run + ep_ref[HALO - d:HALO - d + t, cols]
        sum_lo = run
        for d in range(w_lo, w_hi):
            run = run + ep_ref[HALO - d:HALO - d + t, cols]
        win = jnp.where(lane_lo, w_lo, w_hi)
        cnt = jnp.minimum(tpos, win).astype(F32)
        pooled.append(jnp.where(lane_lo, sum_lo, run) / cnt - cur)
    pooled = jnp.concatenate(pooled, axis=1).astype(BF16)
    ep_ref[0:HALO, :] = ep_ref[t:t + HALO, :]
    mixed_d = _dot(pooled, wpool_ref[...]) * pscale_ref[...]
    gp_a = gate_pre(0)
    y_d = _dot(mixed_d.astype(BF16), wbr_d_ref[...])
    merged = merged + gated(gp_d, y_d)

    y_a = _dot(o_ref[...], wbr_a_ref[...])
    half = t // 2
    for r in range(2):
        rows = slice(r * half, (r + 1) * half)
        m_r = merged[rows] + gated(gp_a[rows], y_a[rows])
        mo = _dot(m_r.astype(BF16), wout_ref[...])
        out_ref[rows, :] = x[rows] + _rms(mo, gpost_ref[...])


def _mixer(l, x2, o, w, seq):
    n = x2.shape[0]
    t = T_MIX
    return pl.pallas_call(
        functools.partial(_mixer_kernel, seq // t),
        grid=(n // t,),
        in_specs=[_row_spec(t, D_MODEL),
                  _row_spec(t, HEADS * VDIM),
                  _layer_spec(l, 1, D_MODEL),
                  _layer_spec(l, D_MODEL, REST_COLS),
                  _layer_spec(l, 1, N_BRANCH * D_MODEL),
                  _layer_spec(l, 1, SG_WIDTH),
                  _layer_spec(l, 1, SG_WIDTH),
                  _layer_spec(l, SG_CHUNK, SG_GROUPS * SG_CHUNK),
                  _layer_spec(l, SG_CHUNK, SG_WIDTH),
                  _layer_spec(l, 3, BR_WIDTH),
                  _layer_spec(l, BR_WIDTH, BR_WIDTH),
                  _layer_spec(l, 1, BR_WIDTH),
                  _layer_spec(l, BR_WIDTH, D_MODEL),
                  _layer_spec(l, BR_WIDTH, D_MODEL),
                  _layer_spec(l, BR_WIDTH, D_MODEL),
                  _layer_spec(l, BR_WIDTH, D_MODEL),
                  _layer_spec(l, D_MODEL, D_MODEL),
                  _layer_spec(l, 1, D_MODEL)],
        out_specs=_row_spec(t, D_MODEL),
        out_shape=jax.ShapeDtypeStruct((n, D_MODEL), F32),
        scratch_shapes=[pltpu.VMEM((t + HALO, BR_WIDTH), F32),
                        pltpu.VMEM((t + HALO, BR_WIDTH), F32)],
        compiler_params=_params(),
        name="mixer",
    )(x2, o, w["norm_mix_pre"], w["w_rest_in"], w["gate_b"], w["sg_ln_g"], w["sg_ln_b"], w["w_sg"],
      w["sg_bias"], w["conv_w"], w["w_pool"], w["pool_scale"], w["w_br_mla"], w["w_br_sg"],
      w["w_br_conv"], w["w_br_pool"], w["w_out"], w["norm_mix_post"])


def _ffn_kernel(x_ref, g_ref, w1_ref, w2_ref, gpost_ref, *rest):
    fused = len(rest) > 1
    out_ref = rest[-4] if fused else rest[0]
    x = x_ref[...]
    h = _rms(x, g_ref[...]).astype(BF16)
    a = jnp.maximum(_dot(h, w1_ref[...]), 0.0)
    a2 = (a * a).astype(BF16)
    half = x.shape[0] // 2
    halves = (slice(0, half), slice(half, 2 * half))
    f_halves = [_dot(a2[rows], w2_ref[...]) for rows in halves]
    for rows, f in zip(halves, f_halves):
        x_new = x[rows] + _rms(f, gpost_ref[...])
        out_ref[rows, :] = x_new
        if fused:
            _attn_proj_body(x_new, rows, *rest[:-4], *rest[-3:])


def _ffn(l, x2, w, tables=None):
    n = x2.shape[0]
    in_specs = [_row_spec(T_FFN, D_MODEL),
                _layer_spec(l, 1, D_MODEL),
                _layer_spec(l, D_MODEL, D_FF),
                _layer_spec(l, D_FF, D_MODEL),
                _layer_spec(l, 1, D_MODEL)]
    operands = [x2, w["norm_ffn_pre"], w["w_ff1"], w["w_ff2"], w["norm_ffn_post"]]
    out_specs = [_row_spec(T_FFN, D_MODEL)]
    out_shape = [jax.ShapeDtypeStruct((n, D_MODEL), F32)]
    if tables is not None:
        p_in, p_ops, p_out, p_shape = _attn_proj_io(l + 1, n, tables, w)
        in_specs += p_in
        operands += p_ops
        out_specs += p_out
        out_shape += p_shape
    return pl.pallas_call(
        _ffn_kernel,
        grid=(n // T_FFN,),
        in_specs=in_specs,
        out_specs=out_specs,
        out_shape=out_shape,
        compiler_params=_params(),
        name="ffn" if tables is None else "ffn_proj",
    )(*operands)


def _gather_cols(w, idx, sign):
    return jnp.take(w, jnp.asarray(idx), axis=-1) * jnp.asarray(sign, F32)


def _w_attn_layout_kernel(wt_ref, wa_ref):
    w = wt_ref[...].T
    r0 = Q_RANK + KV_RANK
    blk = w[:, r0:r0 + HEAD_LANES]
    lane = lax.broadcasted_iota(jnp.int32, blk.shape, 1)
    half = ROPE // 2
    lo = NOPE + ROPE
    rope = jnp.where((lane >= NOPE) & (lane < lo), pltpu.roll(blk, NOPE, 1),
                     jnp.where((lane >= lo) & (lane < lo + half), -pltpu.roll(blk, lo - half, 1),
                               jnp.where(lane >= lo + half, pltpu.roll(blk, lo + half, 1), 0.0)))
    wa_ref[...] = jnp.concatenate([w[:, :r0], rope], axis=1).astype(BF16)


def _w_rest_layout_kernel(wt_ref, wb_ref):
    scale = jnp.where(pl.program_id(1) >= GATE_COLS0 // W_COL_BLOCK, 0.5, 1.0)
    wb_ref[...] = (wt_ref[0] * scale).T.astype(BF16)


def _w_in_layout(w_in):
    w_t = jnp.swapaxes(w_in, 1, 2)
    params = pltpu.CompilerParams(dimension_semantics=("arbitrary", "arbitrary"), vmem_limit_bytes=VMEM_LIMIT)
    w_attn = pl.pallas_call(
        _w_attn_layout_kernel,
        grid=(DEPTH, 1),
        in_specs=[pl.BlockSpec((None, 4 * HEAD_LANES, D_MODEL), lambda l, c: (l, 0, 0))],
        out_specs=pl.BlockSpec((None, D_MODEL, ATTN_IN_COLS), lambda l, c: (l, 0, 0)),
        out_shape=jax.ShapeDtypeStruct((DEPTH, D_MODEL, ATTN_IN_COLS), BF16),
        compiler_params=params,
        name="w_attn_layout",
    )(w_t)
    w_rest = pl.pallas_call(
        _w_rest_layout_kernel,
        grid=(DEPTH, REST_COLS // W_COL_BLOCK),
        in_specs=[pl.BlockSpec((pl.Element(1), pl.Element(W_COL_BLOCK), pl.Element(D_MODEL)),
                               lambda l, c: (l, pl.multiple_of(ATTN_COLS + c * W_COL_BLOCK, 32), 0))],
        out_specs=pl.BlockSpec((None, D_MODEL, W_COL_BLOCK), lambda l, c: (l, 0, c)),
        out_shape=jax.ShapeDtypeStruct((DEPTH, D_MODEL, REST_COLS), BF16),
        compiler_params=params,
        name="w_rest_layout",
    )(w_t)
    return w_attn, w_rest


def _q_layout():
    nq = HEADS * HEAD_LANES
    idx = np.zeros(nq + HEADS * ROPE, np.int32)
    sign = np.zeros(nq + HEADS * ROPE, np.float32)
    half = ROPE // 2
    for h in range(HEADS):
        src = h * (NOPE + ROPE)
        dst = h * HEAD_LANES
        for j in range(NOPE + ROPE):
            idx[dst + j] = src + j
            sign[dst + j] = 1.0
        rot = nq + h * ROPE
        for j in range(half):
            idx[rot + j] = src + NOPE + half + j
            sign[rot + j] = -1.0
            idx[rot + half + j] = src + NOPE + j
            sign[rot + half + j] = 1.0
    return idx, sign


def _k_layout():
    nq = HEADS * HEAD_LANES
    idx = np.zeros(nq, np.int32)
    sign = np.zeros(nq, np.float32)
    for h in range(HEADS):
        src = h * (NOPE + VDIM)
        for j in range(NOPE):
            idx[h * HEAD_LANES + j] = src + j
            sign[h * HEAD_LANES + j] = 1.0
    return idx, sign


def _v_layout():
    idx = np.zeros(HEADS * VDIM, np.int32)
    for h in range(HEADS):
        for j in range(VDIM):
            idx[h * VDIM + j] = h * (NOPE + VDIM) + NOPE + j
    return idx, np.ones(HEADS * VDIM, np.float32)


def _prepare_weights(p):
    L = DEPTH
    row = lambda a: a.reshape(L, 1, a.shape[-1])
    eye = jnp.eye(len(POOL_WINDOWS), dtype=F32)
    w_attn_in, w_rest_in = _w_in_layout(p["w_in"])
    return {
        "norm_mix_pre": row(p["norm_mix_pre"]),
        "w_attn_in": w_attn_in,
        "q_norm": row(p["q_norm"]),
        "w_q_t": jnp.swapaxes(_gather_cols(p["w_uq"], *_q_layout()), 1, 2).astype(BF16),
        "kv_norm": row(p["kv_norm"]),
        "w_k": _gather_cols(p["w_ukv"], *_k_layout()).astype(BF16),
        "w_v_t": jnp.swapaxes(_gather_cols(p["w_ukv"], *_v_layout()), 1, 2).astype(BF16),
        "w_rest_in": w_rest_in,
        "gate_b": row(p["gate_b"]) * 0.5,
        "sg_ln_g": row(p["sg_ln_g"]),
        "sg_ln_b": row(p["sg_ln_b"]),
        "w_sg": jnp.transpose(p["sg_w"], (0, 2, 1, 3)).reshape(L, SG_CHUNK, SG_GROUPS * SG_CHUNK).astype(BF16),
        "sg_bias": jnp.repeat(jnp.transpose(p["sg_b"], (0, 2, 1)), SG_WIDTH // SG_GROUPS, axis=-1),
        "conv_w": p["conv_w"],
        "w_pool": jnp.einsum("lgcd,gh->lgchd", p["pool_w"], eye).reshape(L, BR_WIDTH, BR_WIDTH).astype(BF16),
        "pool_scale": row(p["pool_scale"]),
        "w_br_mla": (p["w_br_mla"] * 0.5).astype(BF16),
        "w_br_sg": (p["w_br_sg"] * 0.5).astype(BF16),
        "w_br_conv": (p["w_br_conv"] * 0.5).astype(BF16),
        "w_br_pool": (p["w_br_pool"] * 0.5).astype(BF16),
        "w_out": p["w_out"].astype(BF16),
        "norm_mix_post": row(p["norm_mix_post"]),
        "norm_ffn_pre": row(p["norm_ffn_pre"]),
        "w_ff1": p["w_ff1"].astype(BF16),
        "w_ff2": p["w_ff2"].astype(BF16),
        "norm_ffn_post": row(p["norm_ffn_post"]),
    }


def kernel(x, positions, norm_mix_pre, w_in, gate_b, q_norm, w_uq, kv_norm, w_ukv, w_br_mla, sg_ln_g, sg_ln_b, sg_w, sg_b, w_br_sg, conv_w, w_br_conv, pool_w, pool_scale, w_br_pool, w_out, norm_mix_post, norm_ffn_pre, w_ff1, w_ff2, norm_ffn_post):
    batch, seq, d = x.shape
    w = _prepare_weights(dict(
        norm_mix_pre=norm_mix_pre, w_in=w_in, gate_b=gate_b, q_norm=q_norm, w_uq=w_uq, kv_norm=kv_norm,
        w_ukv=w_ukv, w_br_mla=w_br_mla, sg_ln_g=sg_ln_g, sg_ln_b=sg_ln_b, sg_w=sg_w, sg_b=sg_b,
        w_br_sg=w_br_sg, conv_w=conv_w, w_br_conv=w_br_conv, pool_w=pool_w, pool_scale=pool_scale,
        w_br_pool=w_br_pool, w_out=w_out, norm_mix_post=norm_mix_post, norm_ffn_pre=norm_ffn_pre,
        w_ff1=w_ff1, w_ff2=w_ff2, norm_ffn_post=norm_ffn_post))
    tables = _rope_tables(positions)
    x2 = x.reshape(batch * seq, d)
    assert T_FFN == T_ATT
    q_t, k, v_t = _attn_proj(0, x2, tables, w)
    for l in range(DEPTH):
        o = _attention(q_t, k, v_t, batch, seq)
        x2 = _mixer(l, x2, o, w, seq)
        if l + 1 < DEPTH:
            x2, q_t, k, v_t = _ffn(l, x2, w, tables)
        else:
            (x2,) = _ffn(l, x2, w)
    return x2.reshape(batch, seq, d)
```
